```python
import jax
import jax.numpy as jnp
from jax import lax
import numpy as np

D_MODEL = 2048
BATCH = 2
SEQ = 8192
DEPTH = 4

N_EVEN = (DEPTH + 1) // 2
N_ODD = DEPTH // 2

CONV_CH = D_MODEL // 2
CONV_WIDTH = 3
ATT_HEADS = 8
ATT_HEAD_DIM = (D_MODEL // 2) // ATT_HEADS
ATT_WIDTH = ATT_HEADS * ATT_HEAD_DIM
DILATED_PAIRS = ((128, 1), (512, 4), (2048, 16))
ROPE_THETA = 10000.0
EVEN_IN = 3 * CONV_CH + 3 * ATT_WIDTH
MLSTM_HEADS = 8
MLSTM_QK_DIM = 128
MLSTM_V_DIM = D_MODEL // MLSTM_HEADS
MLSTM_CHUNK = 128
FORGET_BIAS_INIT = 3.0
ODD_IN = 2 * MLSTM_HEADS * MLSTM_QK_DIM + 2 * MLSTM_HEADS * MLSTM_V_DIM + 2 * MLSTM_HEADS
PEER_HEADS = 8
PEER_NKEYS = 128
PEER_QDIM = 256
PEER_TOPK = 16
PEER_EXPERTS = PEER_NKEYS * PEER_NKEYS
PEER_TOKEN_BLOCK = 128
PLE_DIM = 256
POS_OFFSET_MAX = 1024
ALPHA = (2.0 * DEPTH) ** 0.25
BETA = (8.0 * DEPTH) ** -0.25
LN_EPS = 1e-5

kernel_name = 'hybrid_conv_dilattn_mlstm_peer_deepnorm'


def _layer_norm(x, gain, bias):
    xf = x.astype(jnp.float32)
    mu = jnp.mean(xf, axis=-1, keepdims=True)
    var = jnp.mean(jnp.square(xf - mu), axis=-1, keepdims=True)
    y = (xf - mu) * lax.rsqrt(var + LN_EPS) * gain.astype(jnp.float32) + bias.astype(jnp.float32)
    return y.astype(x.dtype)


def _rope(t, positions):
    hd = t.shape[-1]
    half = hd // 2
    inv_freq = jnp.power(ROPE_THETA, -2.0 * jnp.arange(half, dtype=jnp.float32) / hd)
    ang = positions.astype(jnp.float32)[:, :, None] * inv_freq
    cos = jnp.cos(ang)[:, :, None, :].astype(t.dtype)
    sin = jnp.sin(ang)[:, :, None, :].astype(t.dtype)
    t1, t2 = t[..., :half], t[..., half:]
    return jnp.concatenate([t1 * cos - t2 * sin, t2 * cos + t1 * sin], axis=-1)


def _short_conv(y, w):
    ch = y.shape[-1]
    return lax.conv_general_dilated(
        y, w[:, None, :].astype(y.dtype), window_strides=(1,),
        padding=((CONV_WIDTH - 1, 0),), dimension_numbers=('NWC', 'WIO', 'NWC'),
        feature_group_count=ch)


def _dilated_branch(q, k, v, window, dilation):
    bsz, seq, heads, hd = q.shape
    sub_len = seq // dilation
    span = window // dilation
    blk = span
    padded = -(-sub_len // blk) * blk
    nb = padded // blk
    n = bsz * dilation

    def to_sub(t):
        t = t.reshape(bsz, sub_len, dilation, heads, hd).transpose(0, 2, 1, 3, 4)
        t = t.reshape(n, sub_len, heads, hd)
        t = jnp.pad(t, ((0, 0), (0, padded - sub_len), (0, 0), (0, 0)))
        return t.reshape(n, nb, blk, heads, hd)

    def with_prev(t):
        prev = jnp.concatenate([jnp.zeros_like(t[:, :1]), t[:, :-1]], axis=1)
        return jnp.concatenate([prev, t], axis=2)

    qb = to_sub(q)
    kb = with_prev(to_sub(k))
    vb = with_prev(to_sub(v)).astype(jnp.float32)
    s = jnp.einsum('nbqhd,nbkhd->nbhqk', qb, kb).astype(jnp.float32) * (hd ** -0.5)
    qi = jnp.arange(blk)[:, None]
    kj = jnp.arange(2 * blk)[None, :]
    dist = qi + blk - kj
    band = (dist >= 0) & (dist <= span)
    has_prev = (jnp.arange(nb)[:, None, None] > 0) | (kj >= blk)[None]
    mask = band[None] & has_prev
    s = jnp.where(mask[None, :, None], s, -jnp.inf)
    m = jnp.max(s, axis=-1, keepdims=True)
    e = jnp.exp(s - m)
    den = jnp.sum(e, axis=-1)
    o = jnp.einsum('nbhqk,nbkhd->nbqhd', e, vb) / jnp.swapaxes(den, 2, 3)[..., None]
    lse = jnp.swapaxes(m[..., 0] + jnp.log(den), 2, 3)

    def from_sub(t):
        rest = t.shape[3:]
        t = t.reshape(n, padded, *rest)[:, :sub_len]
        t = t.reshape(bsz, dilation, sub_len, *rest)
        t = jnp.swapaxes(t, 1, 2)
        return t.reshape(bsz, seq, *rest)

    return from_sub(o), from_sub(lse)


def _dilated_attention(q, k, v):
    outs, lses = [], []
    for window, dilation in DILATED_PAIRS:
        o, l = _dilated_branch(q, k, v, window, dilation)
        outs.append(o)
        lses.append(l)
    wts = jax.nn.softmax(jnp.stack(lses), axis=0)
    return jnp.einsum('gbsh,gbshd->bshd', wts, jnp.stack(outs)).astype(q.dtype)


def _even_mixer(x, positions, w_in, conv_w, w_out):
    bsz, seq, _ = x.shape
    h = x @ w_in
    cuts = [CONV_CH, 2 * CONV_CH, 3 * CONV_CH, 3 * CONV_CH + ATT_WIDTH, 3 * CONV_CH + 2 * ATT_WIDTH]
    gate_b, gate_c, xa, q, k, v = jnp.split(h, cuts, axis=-1)
    y_conv = gate_b * _short_conv(gate_c * xa, conv_w)
    q = _rope(q.reshape(bsz, seq, ATT_HEADS, ATT_HEAD_DIM), positions)
    k = _rope(k.reshape(bsz, seq, ATT_HEADS, ATT_HEAD_DIM), positions)
    v = v.reshape(bsz, seq, ATT_HEADS, ATT_HEAD_DIM)
    y_att = _dilated_attention(q, k, v).reshape(bsz, seq, ATT_WIDTH)
    return jnp.concatenate([y_conv, y_att], axis=-1) @ w_out


def _mlstm_mixer(x, w_in, gate_bias, w_out):
    bsz, seq, _ = x.shape
    nh, dk, dv, cl = MLSTM_HEADS, MLSTM_QK_DIM, MLSTM_V_DIM, MLSTM_CHUNK
    nc = seq // cl
    h = x @ w_in
    cuts = [nh * dk, 2 * nh * dk, 2 * nh * dk + nh * dv, 2 * nh * dk + 2 * nh * dv,
            2 * nh * dk + 2 * nh * dv + nh]
    q, k, v, og, ig, fg = jnp.split(h, cuts, axis=-1)

    def chunked(t, d):
        return t.reshape(bsz, nc, cl, nh, d).transpose(0, 3, 1, 2, 4).astype(jnp.float32)

    def gate(t, b):
        t = t.astype(jnp.float32) + b.astype(jnp.float32)
        return t.reshape(bsz, nc, cl, nh).transpose(0, 3, 1, 2)

    q = chunked(q, dk)
    k = chunked(k, dk) * (dk ** -0.5)
    v = chunked(v, dv)
    i_pre = gate(ig, gate_bias[0])
    log_f = jax.nn.log_sigmoid(gate(fg, gate_bias[1]))
    bcum = jnp.cumsum(log_f, axis=-1)
    btot = bcum[..., -1]

    w_loc = btot[..., None] - bcum + i_pre
    m_loc = jnp.max(w_loc, axis=-1)
    a_loc = jnp.exp(w_loc - m_loc[..., None])
    c_loc = jnp.einsum('bhcs,bhcsv,bhcsk->bhcvk', a_loc, v, k)
    n_loc = jnp.einsum('bhcs,bhcsk->bhck', a_loc, k)

    def step(carry, xs):
        c, nvec, m = carry
        bt, ml, cloc, nloc = xs
        m_new = jnp.maximum(bt + m, ml)
        a_old = jnp.exp(bt + m - m_new)
        a_new = jnp.exp(ml - m_new)
        c_new = a_old[..., None, None] * c + a_new[..., None, None] * cloc
        n_new = a_old[..., None] * nvec + a_new[..., None] * nloc
        return (c_new, n_new, m_new), (c, nvec, m)

    init = (jnp.zeros((bsz, nh, dv, dk), jnp.float32),
            jnp.zeros((bsz, nh, dk), jnp.float32),
            jnp.zeros((bsz, nh), jnp.float32))
    xs = (jnp.moveaxis(btot, 2, 0), jnp.moveaxis(m_loc, 2, 0),
          jnp.moveaxis(c_loc, 2, 0), jnp.moveaxis(n_loc, 2, 0))
    _, (c_prev, n_prev, m_prev) = lax.scan(step, init, xs)
    c_prev = jnp.moveaxis(c_prev, 0, 2)
    n_prev = jnp.moveaxis(n_prev, 0, 2)
    m_prev = jnp.moveaxis(m_prev, 0, 2)

    causal = jnp.tril(jnp.ones((cl, cl), dtype=bool))
    dmat = bcum[..., :, None] - bcum[..., None, :] + i_pre[..., None, :]
    dmat = jnp.where(causal, dmat, -jnp.inf)
    g_inter = bcum + m_prev[..., None]
    m_t = jnp.maximum(g_inter, jnp.max(dmat, axis=-1))
    sc = jnp.einsum('bhctk,bhcsk->bhcts', q, k) * jnp.exp(dmat - m_t[..., None])
    a_inter = jnp.exp(g_inter - m_t)
    num = (a_inter[..., None] * jnp.einsum('bhcvk,bhctk->bhctv', c_prev, q)
           + jnp.einsum('bhcts,bhcsv->bhctv', sc, v))
    den = a_inter * jnp.einsum('bhck,bhctk->bhct', n_prev, q) + jnp.sum(sc, axis=-1)
    hh = num / jnp.maximum(jnp.abs(den), jnp.exp(-m_t))[..., None]
    hh = hh.transpose(0, 2, 3, 1, 4).reshape(bsz, seq, nh * dv).astype(x.dtype)
    return (jax.nn.sigmoid(og) * hh) @ w_out


def _peer(x, w_q, sub_keys, u, v):
    bsz, seq, dm = x.shape
    half = PEER_QDIM // 2
    kk = PEER_TOPK
    xb = x.reshape((bsz * seq) // PEER_TOKEN_BLOCK, PEER_TOKEN_BLOCK, dm)

    def block(xt):
        q = (xt @ w_q).reshape(xt.shape[0], PEER_HEADS, PEER_QDIM)
        s1 = jnp.einsum('thd,hnd->thn', q[..., :half], sub_keys[:, 0]).astype(jnp.float32)
        s2 = jnp.einsum('thd,hnd->thn', q[..., half:], sub_keys[:, 1]).astype(jnp.float32)
        v1, i1 = lax.top_k(s1, kk)
        v2, i2 = lax.top_k(s2, kk)
        cand = (v1[..., :, None] + v2[..., None, :]).reshape(xt.shape[0], PEER_HEADS, kk * kk)
        sc, ci = lax.top_k(cand, kk)
        e1 = jnp.take_along_axis(i1, ci // kk, axis=-1)
        e2 = jnp.take_along_axis(i2, ci % kk, axis=-1)
        idx = e1 * PEER_NKEYS + e2
        g = jax.nn.softmax(sc, axis=-1)
        hid = jax.nn.gelu(jnp.einsum('td,thkd->thk', xt, u[idx]).astype(jnp.float32), approximate=False)
        return jnp.einsum('thk,thkd->td', (g * hid).astype(xt.dtype), v[idx])

    return lax.map(block, xb).reshape(bsz, seq, dm)


def setup_inputs(seed: int = 0) -> dict:
    key = jax.random.key(seed)
    ks = jax.random.split(key, 17)

    def nrm(k, shape, scale):
        return jax.random.normal(k, shape, jnp.float32) * scale

    x = nrm(ks[0], (BATCH, SEQ, D_MODEL), 1.0)
    p = nrm(ks[1], (DEPTH, BATCH, SEQ, PLE_DIM), 1.0)
    positions = (jnp.arange(SEQ, dtype=jnp.int32)[None, :]
                 + jax.random.randint(ks[2], (BATCH, 1), 0, POS_OFFSET_MAX, dtype=jnp.int32))
    even_w_in = nrm(ks[3], (N_EVEN, D_MODEL, EVEN_IN), D_MODEL ** -0.5)
    even_conv_w = nrm(ks[4], (N_EVEN, CONV_WIDTH, CONV_CH), CONV_WIDTH ** -0.5)
    even_w_out = nrm(ks[5], (N_EVEN, CONV_CH + ATT_WIDTH, D_MODEL), (CONV_CH + ATT_WIDTH) ** -0.5 * BETA)
    odd_w_in = nrm(ks[6], (N_ODD, D_MODEL, ODD_IN), D_MODEL ** -0.5)
    odd_gate_bias = (nrm(ks[7], (N_ODD, 2, MLSTM_HEADS), 0.1)
                     + jnp.array([0.0, FORGET_BIAS_INIT], jnp.float32)[None, :, None])
    odd_w_out = nrm(ks[8], (N_ODD, MLSTM_HEADS * MLSTM_V_DIM, D_MODEL),
                    (MLSTM_HEADS * MLSTM_V_DIM) ** -0.5 * BETA)
    peer_w_q = nrm(ks[9], (DEPTH, D_MODEL, PEER_HEADS * PEER_QDIM), D_MODEL ** -0.5)
    peer_sub_keys = nrm(ks[10], (DEPTH, PEER_HEADS, 2, PEER_NKEYS, PEER_QDIM // 2), (PEER_QDIM // 2) ** -0.5)
    peer_u = nrm(ks[11], (DEPTH, PEER_EXPERTS, D_MODEL), D_MODEL ** -0.5)
    peer_v = nrm(ks[12], (DEPTH, PEER_EXPERTS, D_MODEL), BETA * PEER_HEADS ** -0.5)
    ple_w_proj = nrm(ks[13], (DEPTH, PLE_DIM, D_MODEL), PLE_DIM ** -0.5)
    ple_w_gate = nrm(ks[14], (DEPTH, D_MODEL, D_MODEL), D_MODEL ** -0.5)
    ln_gain = 1.0 + nrm(ks[15], (DEPTH, 2, D_MODEL), 0.02)
    ln_bias = nrm(ks[16], (DEPTH, 2, D_MODEL), 0.02)
    return {'x': x, 'p': p, 'positions': positions,
            'even_w_in': even_w_in, 'even_conv_w': even_conv_w, 'even_w_out': even_w_out,
            'odd_w_in': odd_w_in, 'odd_gate_bias': odd_gate_bias, 'odd_w_out': odd_w_out,
            'peer_w_q': peer_w_q, 'peer_sub_keys': peer_sub_keys, 'peer_u': peer_u, 'peer_v': peer_v,
            'ple_w_proj': ple_w_proj, 'ple_w_gate': ple_w_gate,
            'ln_gain': ln_gain, 'ln_bias': ln_bias}


def reference(x, p, positions, even_w_in, even_conv_w, even_w_out, odd_w_in, odd_gate_bias,
              odd_w_out, peer_w_q, peer_sub_keys, peer_u, peer_v, ple_w_proj, ple_w_gate,
              ln_gain, ln_bias):
    for layer in range(DEPTH):
        j = layer // 2
        if layer % 2 == 0:
            mix = _even_mixer(x, positions, even_w_in[j], even_conv_w[j], even_w_out[j])
        else:
            mix = _mlstm_mixer(x, odd_w_in[j], odd_gate_bias[j], odd_w_out[j])
        x = _layer_norm(ALPHA * x + mix, ln_gain[layer, 0], ln_bias[layer, 0])
        ffn = _peer(x, peer_w_q[layer], peer_sub_keys[layer], peer_u[layer], peer_v[layer])
        ple = jax.nn.sigmoid(x @ ple_w_gate[layer]) * (p[layer] @ ple_w_proj[layer])
        x = _layer_norm(ALPHA * x + ffn + ple, ln_gain[layer, 1], ln_bias[layer, 1])
    return x
```

```python
import functools
import math

import jax
import jax.numpy as jnp
from jax import lax
from jax.experimental import pallas as pl
from jax.experimental.pallas import tpu as pltpu

D_MODEL = 2048
DEPTH = 4
CONV_CH = D_MODEL // 2
CONV_WIDTH = 3
ATT_HEADS = 8
ATT_HEAD_DIM = 128
ATT_WIDTH = ATT_HEADS * ATT_HEAD_DIM
DILATED_PAIRS = ((128, 1), (512, 4), (2048, 16))
ROPE_THETA = 10000.0
EVEN_IN = 3 * CONV_CH + 3 * ATT_WIDTH
MLSTM_HEADS = 8
MLSTM_QK_DIM = 128
MLSTM_V_DIM = D_MODEL // MLSTM_HEADS
MLSTM_CHUNK = 128
ODD_MAIN = 2 * MLSTM_HEADS * MLSTM_QK_DIM + 2 * MLSTM_HEADS * MLSTM_V_DIM
PEER_HEADS = 8
PEER_NKEYS = 128
PEER_QDIM = 256
PEER_TOPK = 16
PEER_EXPERTS = PEER_NKEYS * PEER_NKEYS
PLE_DIM = 256
ALPHA = (2.0 * DEPTH) ** 0.25
LN_EPS = 1e-5

LANES = 128
SUBLANES = 8
VMEM_LIMIT_BYTES = 56 * 1024 * 1024

ATT_BLOCK = 128
ATT_SPAN = 128
ATT_TOKENS = ATT_BLOCK * max(d for _, d in DILATED_PAIRS)

NEG_INF = float("-inf")
F32 = jnp.float32
BF16 = jnp.bfloat16

_TRANS_B = (((1,), (1,)), ((), ()))


def _params(*semantics):
    return pltpu.CompilerParams(dimension_semantics=semantics,
                                vmem_limit_bytes=VMEM_LIMIT_BYTES)


def _layer_norm_rows(y, gain, bias):
    mu = jnp.mean(y, axis=-1, keepdims=True)
    yc = y - mu
    var = jnp.mean(yc * yc, axis=-1, keepdims=True)
    return yc * lax.rsqrt(var + LN_EPS) * gain + bias


def _mm_kernel(a_ref, w_ref, o_ref):
    o_ref[...] = jnp.dot(a_ref[...], w_ref[...],
                         preferred_element_type=F32).astype(o_ref.dtype)


def _matmul(a, w, out_dtype, tm=512, tn=None):
    m, k = a.shape
    n = w.shape[1]
    tn = min(n, 2048) if tn is None else tn
    return pl.pallas_call(
        _mm_kernel,
        out_shape=jax.ShapeDtypeStruct((m, n), out_dtype),
        grid=(n // tn, m // tm),
        in_specs=[pl.BlockSpec((tm, k), lambda j, i: (i, 0)),
                  pl.BlockSpec((k, tn), lambda j, i: (0, j))],
        out_specs=pl.BlockSpec((tm, tn), lambda j, i: (i, j)),
        compiler_params=_params("parallel", "parallel"),
        name="matmul",
    )(a, w)


def _mm_ln_kernel(*refs, n_a):
    a_refs = refs[:n_a]
    w_ref, res_ref, g_ref, b_ref, o_ref, ob_ref = refs[n_a:]
    acc = None
    off = 0
    for a_ref in a_refs:
        k = a_ref.shape[1]
        part = jnp.dot(a_ref[...], w_ref[off:off + k, :], preferred_element_type=F32)
        acc = part if acc is None else acc + part
        off += k
    y = ALPHA * res_ref[...] + acc
    out = _layer_norm_rows(y, g_ref[...], b_ref[...])
    o_ref[...] = out
    ob_ref[...] = out.astype(BF16)


def _matmul_residual_ln(a_list, w, res, gain, bias, tm=256):
    m = res.shape[0]
    n = w.shape[1]
    in_specs = [pl.BlockSpec((tm, a.shape[1]), lambda i: (i, 0)) for a in a_list]
    in_specs += [pl.BlockSpec(w.shape, lambda i: (0, 0)),
                 pl.BlockSpec((tm, n), lambda i: (i, 0)),
                 pl.BlockSpec((1, n), lambda i: (0, 0)),
                 pl.BlockSpec((1, n), lambda i: (0, 0))]
    return pl.pallas_call(
        functools.partial(_mm_ln_kernel, n_a=len(a_list)),
        out_shape=(jax.ShapeDtypeStruct((m, n), F32), jax.ShapeDtypeStruct((m, n), BF16)),
        grid=(m // tm,),
        in_specs=in_specs,
        out_specs=(pl.BlockSpec((tm, n), lambda i: (i, 0)),
                   pl.BlockSpec((tm, n), lambda i: (i, 0))),
        compiler_params=_params("parallel"),
        name="matmul_residual_ln",
    )(*a_list, w, res, gain.reshape(1, n), bias.reshape(1, n))


def _ple_ln_kernel(x_ref, xb_ref, ffn_ref, p_ref, wg_ref, wp_ref, g_ref, b_ref, o_ref, ob_ref):
    gate = jax.nn.sigmoid(jnp.dot(xb_ref[...], wg_ref[...], preferred_element_type=F32))
    proj = jnp.dot(p_ref[...], wp_ref[...], preferred_element_type=F32)
    y = ALPHA * x_ref[...] + ffn_ref[...] + gate * proj
    out = _layer_norm_rows(y, g_ref[...], b_ref[...])
    o_ref[...] = out
    ob_ref[...] = out.astype(BF16)


def _ple_residual_ln(x, xb, ffn, p, wg, wp, gain, bias, tm=256):
    m, n = x.shape
    row = lambda i: (i, 0)
    const = lambda i: (0, 0)
    return pl.pallas_call(
        _ple_ln_kernel,
        out_shape=(jax.ShapeDtypeStruct((m, n), F32), jax.ShapeDtypeStruct((m, n), BF16)),
        grid=(m // tm,),
        in_specs=[pl.BlockSpec((tm, n), row), pl.BlockSpec((tm, n), row),
                  pl.BlockSpec((tm, n), row), pl.BlockSpec((tm, p.shape[1]), row),
                  pl.BlockSpec(wg.shape, const), pl.BlockSpec(wp.shape, const),
                  pl.BlockSpec((1, n), const), pl.BlockSpec((1, n), const)],
        out_specs=(pl.BlockSpec((tm, n), row), pl.BlockSpec((tm, n), row)),
        compiler_params=_params("parallel"),
        name="ple_residual_ln",
    )(x, xb, ffn, p, wg, wp, gain.reshape(1, n), bias.reshape(1, n))


def _conv_gate_kernel(gb_ref, gc_ref, xa_ref, w_ref, o_ref, buf, *, steps_per_seq):
    tm = gb_ref.shape[0]
    halo = SUBLANES

    @pl.when(pl.program_id(0) % steps_per_seq == 0)
    def _():
        buf[0:halo, :] = jnp.zeros((halo, buf.shape[1]), F32)

    u = gc_ref[...] * xa_ref[...]
    buf[halo:halo + tm, :] = u
    w = w_ref[...]
    conv = (w[0:1, :] * buf[halo - 2:halo - 2 + tm, :]
            + w[1:2, :] * buf[halo - 1:halo - 1 + tm, :]
            + w[2:3, :] * u)
    o_ref[...] = (gb_ref[...] * conv).astype(o_ref.dtype)
    buf[0:halo, :] = buf[tm:tm + halo, :]


def _conv_gate(h, conv_w, seq, tm=512):
    t = h.shape[0]
    ch = CONV_CH
    return pl.pallas_call(
        functools.partial(_conv_gate_kernel, steps_per_seq=seq // tm),
        out_shape=jax.ShapeDtypeStruct((t, ch), BF16),
        grid=(t // tm,),
        in_specs=[pl.BlockSpec((tm, ch), lambda i: (i, 0)),
                  pl.BlockSpec((tm, ch), lambda i: (i, 1)),
                  pl.BlockSpec((tm, ch), lambda i: (i, 2)),
                  pl.BlockSpec((CONV_WIDTH, ch), lambda i: (0, 0))],
        out_specs=pl.BlockSpec((tm, ch), lambda i: (i, 0)),
        scratch_shapes=[pltpu.VMEM((tm + SUBLANES, ch), F32)],
        compiler_params=_params("arbitrary"),
        name="conv_gate",
    )(h, h, h, conv_w)


def _rope(t, cosf, sinf):
    return t * cosf + pltpu.roll(t, ATT_HEAD_DIM // 2, 1) * sinf


def _attn_kernel(q_ref, kc_ref, kp_ref, vc_ref, vp_ref, cc_ref, sc_ref, cp_ref, sp_ref,
                 o_ref, q_s, k_s, v_s, acc_s, m_s, l_s):
    j = pl.program_id(2)
    bl = ATT_TOKENS
    blk = ATT_BLOCK
    cc = cc_ref[...]
    sc = sc_ref[...]
    q_s[...] = _rope(q_ref[...], cc, sc)
    k_s[bl:2 * bl, :] = _rope(kc_ref[...], cc, sc)
    k_s[0:bl, :] = _rope(kp_ref[...], cp_ref[...], sp_ref[...])
    v_s[bl:2 * bl, :] = vc_ref[...]
    v_s[0:bl, :] = vp_ref[...]

    qi = lax.broadcasted_iota(jnp.int32, (blk, 2 * blk), 0)
    kj = lax.broadcasted_iota(jnp.int32, (blk, 2 * blk), 1)
    dist = qi + blk - kj
    band = (dist >= 0) & (dist <= ATT_SPAN)
    band_first = band & (kj >= jnp.where(j > 0, 0, blk))
    scale = ATT_HEAD_DIM ** -0.5

    for branch, (_, d) in enumerate(DILATED_PAIRS):
        for g in range(bl // (blk * d)):
            for r in range(d):
                q_rows = pl.ds(g * blk * d + r, blk, stride=d) if d > 1 else pl.ds(g * blk, blk)
                k_start = bl + (g - 1) * blk * d + r
                k_rows = pl.ds(k_start, 2 * blk, stride=d) if d > 1 else pl.ds(k_start, 2 * blk)
                qs = q_s[q_rows, :].astype(BF16)
                ks = k_s[k_rows, :].astype(BF16)
                vs = v_s[k_rows, :].astype(BF16)
                s = lax.dot_general(qs, ks, _TRANS_B, preferred_element_type=F32) * scale
                s = jnp.where(band_first if g == 0 else band, s, NEG_INF)
                m_b = jnp.max(s, axis=-1, keepdims=True)
                e = jnp.exp(s - m_b)
                l_b = jnp.sum(e, axis=-1, keepdims=True)
                o_b = jnp.dot(e.astype(BF16), vs, preferred_element_type=F32)
                if branch == 0:
                    acc_s[q_rows, :] = o_b
                    m_s[q_rows, :] = jnp.broadcast_to(m_b, (blk, LANES))
                    l_s[q_rows, :] = jnp.broadcast_to(l_b, (blk, LANES))
                else:
                    m_old = m_s[q_rows, :]
                    m_new = jnp.maximum(m_old, m_b)
                    a_old = jnp.exp(m_old - m_new)
                    a_new = jnp.exp(m_b - m_new)
                    acc_s[q_rows, :] = a_old * acc_s[q_rows, :] + a_new * o_b
                    l_s[q_rows, :] = a_old * l_s[q_rows, :] + a_new * l_b
                    m_s[q_rows, :] = m_new

    o_ref[...] = (acc_s[...] / l_s[...]).astype(o_ref.dtype)


def _dilated_attention(h, cosf, sinf, batch, seq):
    t = h.shape[0]
    bl = ATT_TOKENS
    nj = seq // bl
    hd = ATT_HEAD_DIM
    q_col = 3 * CONV_CH // hd
    k_col = q_col + ATT_HEADS
    v_col = k_col + ATT_HEADS

    def cur(col):
        return lambda b, hh, j: (b * nj + j, col + hh)

    def prev(col):
        return lambda b, hh, j: (b * nj + jnp.maximum(j - 1, 0), col + hh)

    rows_cur = lambda b, hh, j: (b * nj + j, 0)
    rows_prev = lambda b, hh, j: (b * nj + jnp.maximum(j - 1, 0), 0)
    blk = (bl, hd)
    return pl.pallas_call(
        _attn_kernel,
        out_shape=jax.ShapeDtypeStruct((t, ATT_WIDTH), BF16),
        grid=(batch, ATT_HEADS, nj),
        in_specs=[pl.BlockSpec(blk, cur(q_col)),
                  pl.BlockSpec(blk, cur(k_col)), pl.BlockSpec(blk, prev(k_col)),
                  pl.BlockSpec(blk, cur(v_col)), pl.BlockSpec(blk, prev(v_col)),
                  pl.BlockSpec(blk, rows_cur), pl.BlockSpec(blk, rows_cur),
                  pl.BlockSpec(blk, rows_prev), pl.BlockSpec(blk, rows_prev)],
        out_specs=pl.BlockSpec(blk, lambda b, hh, j: (b * nj + j, hh)),
        scratch_shapes=[pltpu.VMEM((bl, hd), F32), pltpu.VMEM((2 * bl, hd), F32),
                        pltpu.VMEM((2 * bl, hd), F32), pltpu.VMEM((bl, hd), F32),
                        pltpu.VMEM((bl, LANES), F32), pltpu.VMEM((bl, LANES), F32)],
        compiler_params=_params("parallel", "parallel", "parallel"),
        name="dilated_attention",
    )(h, h, h, h, h, cosf, sinf, cosf, sinf)


def _log_sigmoid(x):
    return jnp.minimum(x, 0.0) - jnp.log(1.0 + jnp.exp(-jnp.abs(x)))


def _mlstm_kernel(q_ref, k_ref, v_ref, og_ref, g_ref, bias_ref, o_ref, c_s, n_s, m_s):
    nh, dk, dv, cl = MLSTM_HEADS, MLSTM_QK_DIM, MLSTM_V_DIM, MLSTM_CHUNK

    @pl.when(pl.program_id(1) == 0)
    def _():
        c_s[...] = jnp.zeros(c_s.shape, F32)
        n_s[...] = jnp.zeros(n_s.shape, F32)
        m_s[...] = jnp.zeros(m_s.shape, F32)

    gates = g_ref[...] + bias_ref[...]
    gates_t = gates.T
    row = lax.broadcasted_iota(jnp.int32, (cl, cl), 0)
    col = lax.broadcasted_iota(jnp.int32, (cl, cl), 1)
    causal = col <= row

    for hd in range(nh):
        ig_col = gates[:, hd:hd + 1]
        ig_row = gates_t[hd:hd + 1, :]
        lf_col = _log_sigmoid(gates[:, nh + hd:nh + hd + 1])
        lf_row = _log_sigmoid(gates_t[nh + hd:nh + hd + 1, :])
        bcum_col = jnp.sum(jnp.where(causal, lf_row, 0.0), axis=1, keepdims=True)
        bcum_row = jnp.sum(jnp.where(row <= col, lf_col, 0.0), axis=0, keepdims=True)
        btot = bcum_col[cl - 1:cl, :]

        qf = q_ref[:, hd * dk:(hd + 1) * dk]
        kf = k_ref[:, hd * dk:(hd + 1) * dk] * (dk ** -0.5)
        vf = v_ref[:, hd * dv:(hd + 1) * dv]
        qb = qf.astype(BF16)
        kb = kf.astype(BF16)
        c_prev = c_s[hd]
        n_prev = n_s[hd]
        m_prev = m_s[hd][:, 0:1]

        dmat = jnp.where(causal, bcum_col - bcum_row + ig_row, NEG_INF)
        g_inter = bcum_col + m_prev
        m_t = jnp.maximum(g_inter, jnp.max(dmat, axis=1, keepdims=True))
        qk = lax.dot_general(qb, kb, _TRANS_B, preferred_element_type=F32)
        sc = qk * jnp.exp(dmat - m_t)
        a_inter = jnp.exp(g_inter - m_t)
        num = (a_inter * lax.dot_general(qb, c_prev.astype(BF16), _TRANS_B,
                                         preferred_element_type=F32)
               + jnp.dot(sc.astype(BF16), vf.astype(BF16), preferred_element_type=F32))
        den = (a_inter * jnp.sum(qf * n_prev, axis=1, keepdims=True)
               + jnp.sum(sc, axis=1, keepdims=True))
        hh = num / jnp.maximum(jnp.abs(den), jnp.exp(-m_t))
        o_ref[:, hd * dv:(hd + 1) * dv] = (
            jax.nn.sigmoid(og_ref[:, hd * dv:(hd + 1) * dv]) * hh).astype(o_ref.dtype)

        w_loc = btot - bcum_col + ig_col
        m_loc = jnp.max(w_loc, axis=0, keepdims=True)
        a_loc = jnp.exp(w_loc - m_loc)
        c_loc = jnp.dot((vf * a_loc).T.astype(BF16), kb, preferred_element_type=F32)
        n_loc = jnp.sum(a_loc * kf, axis=0, keepdims=True)
        m_new = jnp.maximum(btot + m_prev, m_loc)
        a_old = jnp.exp(btot + m_prev - m_new)
        a_new = jnp.exp(m_loc - m_new)
        c_s[hd] = a_old * c_prev + a_new * c_loc
        n_s[hd] = a_old * n_prev + a_new * n_loc
        m_s[hd] = jnp.broadcast_to(m_new, (1, LANES))


def _mlstm(h, gates, gate_bias_row, batch, seq):
    t = h.shape[0]
    nh, dk, dv, cl = MLSTM_HEADS, MLSTM_QK_DIM, MLSTM_V_DIM, MLSTM_CHUNK
    nc = seq // cl
    rows = lambda col: (lambda b, c: (b * nc + c, col))
    return pl.pallas_call(
        _mlstm_kernel,
        out_shape=jax.ShapeDtypeStruct((t, nh * dv), BF16),
        grid=(batch, nc),
        in_specs=[pl.BlockSpec((cl, nh * dk), rows(0)),
                  pl.BlockSpec((cl, nh * dk), rows(1)),
                  pl.BlockSpec((cl, nh * dv), rows(1)),
                  pl.BlockSpec((cl, nh * dv), rows(2)),
                  pl.BlockSpec((cl, LANES), rows(0)),
                  pl.BlockSpec((1, LANES), lambda b, c: (0, 0))],
        out_specs=pl.BlockSpec((cl, nh * dv), rows(0)),
        scratch_shapes=[pltpu.VMEM((nh, dv, dk), F32), pltpu.VMEM((nh, 1, dk), F32),
                        pltpu.VMEM((nh, 1, LANES), F32)],
        compiler_params=_params("parallel", "arbitrary"),
        name="mlstm",
    )(h, h, h, h, gates, gate_bias_row)


_STAT_THR, _STAT_M1, _STAT_M2, _STAT_DEN = 0, 1, 2, 3


def _top_values(s, k):
    vals = []
    for _ in range(k):
        mx = jnp.max(s, axis=0, keepdims=True)
        vals.append(mx)
        s = jnp.where(s == mx, NEG_INF, s)
    return vals


def _peer_topk_kernel(q_ref, keys_ref, s1_ref, s2_ref, st_ref):
    kk = PEER_TOPK
    half = PEER_QDIM // 2
    tb = q_ref.shape[0]
    q = q_ref[...]
    s1_ref[0] = lax.dot_general(keys_ref[0, 0], q[:, :half], _TRANS_B, preferred_element_type=F32)
    s2_ref[0] = lax.dot_general(keys_ref[0, 1], q[:, half:], _TRANS_B, preferred_element_type=F32)

    def chunk(c, carry):
        lanes = pl.ds(pl.multiple_of(c * LANES, LANES), LANES)
        v1 = _top_values(s1_ref[0, :, lanes], kk)
        v2 = _top_values(s2_ref[0, :, lanes], kk)
        v2_all = jnp.concatenate(v2, axis=0)
        slabs = [v1[0] + v2_all]
        slabs += [v1[a] + v2_all[0:kk // 2] for a in range(1, kk // 2)]
        slabs += [jnp.concatenate(v1[kk // 2:], axis=0) + v2[0]]
        cand = jnp.concatenate(slabs, axis=0)
        m = v1[0] + v2[0]
        den = jnp.zeros_like(m)
        thr = m
        for _ in range(kk):
            thr = jnp.max(cand, axis=0, keepdims=True)
            den = den + jnp.exp(thr - m)
            cand = jnp.where(cand == thr, NEG_INF, cand)
        st_ref[0, :, lanes] = jnp.concatenate(
            [thr, v1[0], v2[0], den, jnp.zeros((SUBLANES - 4, LANES), F32)], axis=0)
        return carry

    lax.fori_loop(0, tb // LANES, chunk, 0)


def _peer_topk(qb, keys_b, tb=512):
    t = qb.shape[0]
    nh, nk = PEER_HEADS, PEER_NKEYS
    return pl.pallas_call(
        _peer_topk_kernel,
        out_shape=(jax.ShapeDtypeStruct((nh, nk, t), F32),
                   jax.ShapeDtypeStruct((nh, nk, t), F32),
                   jax.ShapeDtypeStruct((nh, SUBLANES, t), F32)),
        grid=(t // tb, nh),
        in_specs=[pl.BlockSpec((tb, PEER_QDIM), lambda i, h: (i, h)),
                  pl.BlockSpec((1, 2, nk, PEER_QDIM // 2), lambda i, h: (h, 0, 0, 0))],
        out_specs=(pl.BlockSpec((1, nk, tb), lambda i, h: (h, 0, i)),
                   pl.BlockSpec((1, nk, tb), lambda i, h: (h, 0, i)),
                   pl.BlockSpec((1, SUBLANES, tb), lambda i, h: (h, 0, i))),
        compiler_params=_params("parallel", "parallel"),
        name="peer_topk",
    )(qb, keys_b)


def _gelu_exact(x):
    return 0.5 * x * (1.0 + lax.erf(x * (2.0 ** -0.5)))


def _peer_dense_kernel(x_ref, u_ref, vt_ref, s1_ref, s2_ref, st_ref, o_ref,
                       acc_s, hid_s, act_s, p2_s):
    nh, nk = PEER_HEADS, PEER_NKEYS
    te, tb = hid_s.shape
    rows_per_tile = te // nk
    n_lane_chunks = tb // LANES
    e = pl.program_id(1)

    @pl.when(e == 0)
    def _():
        acc_s[...] = jnp.zeros(acc_s.shape, F32)
        for h in range(nh):
            m2 = st_ref[h, _STAT_M2:_STAT_M2 + 1, :]
            den = st_ref[h, _STAT_DEN:_STAT_DEN + 1, :]
            p2_s[h] = jnp.exp(s2_ref[h] - m2) / den

    hid_s[...] = lax.dot_general(u_ref[...], x_ref[...], _TRANS_B, preferred_element_type=F32)

    def slab(idx, carry):
        le1 = idx // n_lane_chunks
        c = idx % n_lane_chunks
        e1 = e * rows_per_tile + le1
        lanes = pl.ds(pl.multiple_of(c * LANES, LANES), LANES)
        rows = pl.ds(pl.multiple_of(le1 * nk, nk), nk)
        w = jnp.zeros((nk, LANES), F32)
        e1_group = pl.ds(pl.multiple_of((e1 // SUBLANES) * SUBLANES, SUBLANES), SUBLANES)
        e1_pick = lax.broadcasted_iota(jnp.int32, (SUBLANES, LANES), 0) == e1 % SUBLANES
        for h in range(nh):
            s1_row = jnp.sum(jnp.where(e1_pick, s1_ref[h, e1_group, lanes], 0.0),
                             axis=0, keepdims=True)
            thr = st_ref[h, _STAT_THR:_STAT_THR + 1, lanes]
            p1_row = jnp.exp(s1_row - st_ref[h, _STAT_M1:_STAT_M1 + 1, lanes])
            keep = (s1_row + s2_ref[h, :, lanes]) >= thr
            w = w + jnp.where(keep, p1_row * p2_s[h, :, lanes], 0.0)
        act_s[rows, lanes] = (w * _gelu_exact(hid_s[rows, lanes])).astype(BF16)
        return carry

    lax.fori_loop(0, rows_per_tile * n_lane_chunks, slab, 0)
    acc_s[...] += jnp.dot(vt_ref[...], act_s[...], preferred_element_type=F32)

    @pl.when(e == pl.num_programs(1) - 1)
    def _():
        o_ref[...] = acc_s[...].T


def _peer_dense(xb, u_b, vt_b, s1, s2, stats, tb=512, te=512):
    t, dm = xb.shape
    nh, nk = PEER_HEADS, PEER_NKEYS
    ne = u_b.shape[0]
    return pl.pallas_call(
        _peer_dense_kernel,
        out_shape=jax.ShapeDtypeStruct((t, dm), F32),
        grid=(t // tb, ne // te),
        in_specs=[pl.BlockSpec((tb, dm), lambda i, e: (i, 0)),
                  pl.BlockSpec((te, dm), lambda i, e: (e, 0)),
                  pl.BlockSpec((dm, te), lambda i, e: (0, e)),
                  pl.BlockSpec((nh, nk, tb), lambda i, e: (0, 0, i)),
                  pl.BlockSpec((nh, nk, tb), lambda i, e: (0, 0, i)),
                  pl.BlockSpec((nh, SUBLANES, tb), lambda i, e: (0, 0, i))],
        out_specs=pl.BlockSpec((tb, dm), lambda i, e: (i, 0)),
        scratch_shapes=[pltpu.VMEM((dm, tb), F32), pltpu.VMEM((te, tb), F32),
                        pltpu.VMEM((te, tb), BF16), pltpu.VMEM((nh, nk, tb), F32)],
        compiler_params=_params("parallel", "arbitrary"),
        name="peer_dense",
    )(xb, u_b, vt_b, s1, s2, stats)


def _peer(xb, w_q_b, keys_b, u_b, vt_b):
    qb = _matmul(xb, w_q_b, BF16)
    s1, s2, stats = _peer_topk(qb, keys_b)
    return _peer_dense(xb, u_b, vt_b, s1, s2, stats)


def _rope_tables(positions):
    half = ATT_HEAD_DIM // 2
    inv_freq = jnp.power(ROPE_THETA, -2.0 * jnp.arange(half, dtype=F32) / ATT_HEAD_DIM)
    ang = positions.astype(F32).reshape(-1, 1) * inv_freq
    cos, sin = jnp.cos(ang), jnp.sin(ang)
    return jnp.concatenate([cos, cos], axis=-1), jnp.concatenate([-sin, sin], axis=-1)


def kernel(x, p, positions, even_w_in, even_conv_w, even_w_out, odd_w_in, odd_gate_bias,
           odd_w_out, peer_w_q, peer_sub_keys, peer_u, peer_v, ple_w_proj, ple_w_gate,
           ln_gain, ln_bias):
    batch, seq, dm = x.shape
    t = batch * seq
    assert dm == D_MODEL and seq % ATT_TOKENS == 0 and seq % MLSTM_CHUNK == 0
    xf = x.reshape(t, dm)
    xb = xf.astype(BF16)
    cosf, sinf = _rope_tables(positions)

    for layer in range(DEPTH):
        j = layer // 2
        if layer % 2 == 0:
            h = _matmul(xb, even_w_in[j].astype(BF16), F32)
            y_conv = _conv_gate(h, even_conv_w[j], seq)
            y_att = _dilated_attention(h, cosf, sinf, batch, seq)
            mixed = [y_conv, y_att]
            w_out = even_w_out[j]
        else:
            w_in = odd_w_in[j]
            h = _matmul(xb, w_in[:, :ODD_MAIN].astype(BF16), F32)
            n_gates = 2 * MLSTM_HEADS
            w_gates = jnp.pad(w_in[:, ODD_MAIN:], ((0, 0), (0, LANES - n_gates))).astype(BF16)
            gates = _matmul(xb, w_gates, F32)
            bias_row = jnp.pad(odd_gate_bias[j].reshape(1, n_gates), ((0, 0), (0, LANES - n_gates)))
            mixed = [_mlstm(h, gates, bias_row, batch, seq)]
            w_out = odd_w_out[j]
        xf, xb = _matmul_residual_ln(mixed, w_out.astype(BF16), xf,
                                     ln_gain[layer, 0], ln_bias[layer, 0])
        ffn = _peer(xb, peer_w_q[layer].astype(BF16), peer_sub_keys[layer].astype(BF16),
                    peer_u[layer].astype(BF16), peer_v[layer].astype(BF16).T)
        xf, xb = _ple_residual_ln(xf, xb, ffn, p[layer].reshape(t, PLE_DIM).astype(BF16),
                                  ple_w_gate[layer].astype(BF16), ple_w_proj[layer].astype(BF16),
                                  ln_gain[layer, 1], ln_bias[layer, 1])
    return xf.reshape(batch, seq, dm)
```

```python
import functools
import math

import jax
import jax.numpy as jnp
from jax import lax
from jax.experimental import pallas as pl
from jax.experimental.pallas import tpu as pltpu

D_MODEL = 2048
DEPTH = 4
CONV_CH = D_MODEL // 2
CONV_WIDTH = 3
ATT_HEADS = 8
ATT_HEAD_DIM = 128
ATT_WIDTH = ATT_HEADS * ATT_HEAD_DIM
DILATED_PAIRS = ((128, 1), (512, 4), (2048, 16))
ROPE_THETA = 10000.0
EVEN_IN = 3 * CONV_CH + 3 * ATT_WIDTH
MLSTM_HEADS = 8
MLSTM_QK_DIM = 128
MLSTM_V_DIM = D_MODEL // MLSTM_HEADS
MLSTM_CHUNK = 128
ODD_MAIN = 2 * MLSTM_HEADS * MLSTM_QK_DIM + 2 * MLSTM_HEADS * MLSTM_V_DIM
PEER_HEADS = 8
PEER_NKEYS = 128
PEER_QDIM = 256
PEER_TOPK = 16
PEER_EXPERTS = PEER_NKEYS * PEER_NKEYS
PLE_DIM = 256
ALPHA = (2.0 * DEPTH) ** 0.25
LN_EPS = 1e-5

LANES = 128
SUBLANES = 8
VMEM_LIMIT_BYTES = 56 * 1024 * 1024

ATT_BLOCK = 128
ATT_SPAN = 128
ATT_TOKENS = ATT_BLOCK * max(d for _, d in DILATED_PAIRS)

NEG_INF = float("-inf")
F32 = jnp.float32
BF16 = jnp.bfloat16

_TRANS_B = (((1,), (1,)), ((), ()))


def _params(*semantics):
    return pltpu.CompilerParams(dimension_semantics=semantics,
                                vmem_limit_bytes=VMEM_LIMIT_BYTES)


def _layer_norm_rows(y, gain, bias):
    mu = jnp.mean(y, axis=-1, keepdims=True)
    yc = y - mu
    var = jnp.mean(yc * yc, axis=-1, keepdims=True)
    return yc * lax.rsqrt(var + LN_EPS) * gain + bias


def _mm_kernel(a_ref, w_ref, o_ref):
    o_ref[...] = jnp.dot(a_ref[...], w_ref[...],
                         preferred_element_type=F32).astype(o_ref.dtype)


def _matmul(a, w, out_dtype, tm=512, tn=None):
    m, k = a.shape
    n = w.shape[1]
    tn = min(n, 2048) if tn is None else tn
    return pl.pallas_call(
        _mm_kernel,
        out_shape=jax.ShapeDtypeStruct((m, n), out_dtype),
        grid=(n // tn, m // tm),
        in_specs=[pl.BlockSpec((tm, k), lambda j, i: (i, 0)),
                  pl.BlockSpec((k, tn), lambda j, i: (0, j))],
        out_specs=pl.BlockSpec((tm, tn), lambda j, i: (i, j)),
        compiler_params=_params("parallel", "parallel"),
        name="matmul",
    )(a, w)


def _mm_ln_kernel(*refs, n_a):
    a_refs = refs[:n_a]
    w_ref, res_ref, g_ref, b_ref, o_ref, ob_ref = refs[n_a:]
    acc = None
    off = 0
    for a_ref in a_refs:
        k = a_ref.shape[1]
        part = jnp.dot(a_ref[...], w_ref[off:off + k, :], preferred_element_type=F32)
        acc = part if acc is None else acc + part
        off += k
    y = ALPHA * res_ref[...] + acc
    out = _layer_norm_rows(y, g_ref[...], b_ref[...])
    o_ref[...] = out
    ob_ref[...] = out.astype(BF16)


def _matmul_residual_ln(a_list, w, res, gain, bias, tm=256):
    m = res.shape[0]
    n = w.shape[1]
    in_specs = [pl.BlockSpec((tm, a.shape[1]), lambda i: (i, 0)) for a in a_list]
    in_specs += [pl.BlockSpec(w.shape, lambda i: (0, 0)),
                 pl.BlockSpec((tm, n), lambda i: (i, 0)),
                 pl.BlockSpec((1, n), lambda i: (0, 0)),
                 pl.BlockSpec((1, n), lambda i: (0, 0))]
    return pl.pallas_call(
        functools.partial(_mm_ln_kernel, n_a=len(a_list)),
        out_shape=(jax.ShapeDtypeStruct((m, n), F32), jax.ShapeDtypeStruct((m, n), BF16)),
        grid=(m // tm,),
        in_specs=in_specs,
        out_specs=(pl.BlockSpec((tm, n), lambda i: (i, 0)),
                   pl.BlockSpec((tm, n), lambda i: (i, 0))),
        compiler_params=_params("parallel"),
        name="matmul_residual_ln",
    )(*a_list, w, res, gain.reshape(1, n), bias.reshape(1, n))


def _ple_ln_kernel(x_ref, xb_ref, ffn_ref, p_ref, wg_ref, wp_ref, g_ref, b_ref, o_ref, ob_ref):
    gate = jax.nn.sigmoid(jnp.dot(xb_ref[...], wg_ref[...], preferred_element_type=F32))
    proj = jnp.dot(p_ref[...], wp_ref[...], preferred_element_type=F32)
    y = ALPHA * x_ref[...] + ffn_ref[...] + gate * proj
    out = _layer_norm_rows(y, g_ref[...], b_ref[...])
    o_ref[...] = out
    ob_ref[...] = out.astype(BF16)


def _ple_residual_ln(x, xb, ffn, p, wg, wp, gain, bias, tm=256):
    m, n = x.shape
    row = lambda i: (i, 0)
    const = lambda i: (0, 0)
    return pl.pallas_call(
        _ple_ln_kernel,
        out_shape=(jax.ShapeDtypeStruct((m, n), F32), jax.ShapeDtypeStruct((m, n), BF16)),
        grid=(m // tm,),
        in_specs=[pl.BlockSpec((tm, n), row), pl.BlockSpec((tm, n), row),
                  pl.BlockSpec((tm, n), row), pl.BlockSpec((tm, p.shape[1]), row),
                  pl.BlockSpec(wg.shape, const), pl.BlockSpec(wp.shape, const),
                  pl.BlockSpec((1, n), const), pl.BlockSpec((1, n), const)],
        out_specs=(pl.BlockSpec((tm, n), row), pl.BlockSpec((tm, n), row)),
        compiler_params=_params("parallel"),
        name="ple_residual_ln",
    )(x, xb, ffn, p, wg, wp, gain.reshape(1, n), bias.reshape(1, n))


def _conv_gate_kernel(gb_ref, gc_ref, xa_ref, w_ref, o_ref, buf, *, steps_per_seq):
    tm = gb_ref.shape[0]
    halo = SUBLANES

    @pl.when(pl.program_id(0) % steps_per_seq == 0)
    def _():
        buf[0:halo, :] = jnp.zeros((halo, buf.shape[1]), F32)

    u = gc_ref[...] * xa_ref[...]
    buf[halo:halo + tm, :] = u
    w = w_ref[...]
    conv = (w[0:1, :] * buf[halo - 2:halo - 2 + tm, :]
            + w[1:2, :] * buf[halo - 1:halo - 1 + tm, :]
            + w[2:3, :] * u)
    o_ref[...] = (gb_ref[...] * conv).astype(o_ref.dtype)
    buf[0:halo, :] = buf[tm:tm + halo, :]


def _conv_gate(h, conv_w, seq, tm=512):
    t = h.shape[0]
    ch = CONV_CH
    return pl.pallas_call(
        functools.partial(_conv_gate_kernel, steps_per_seq=seq // tm),
        out_shape=jax.ShapeDtypeStruct((t, ch), BF16),
        grid=(t // tm,),
        in_specs=[pl.BlockSpec((tm, ch), lambda i: (i, 0)),
                  pl.BlockSpec((tm, ch), lambda i: (i, 1)),
                  pl.BlockSpec((tm, ch), lambda i: (i, 2)),
                  pl.BlockSpec((CONV_WIDTH, ch), lambda i: (0, 0))],
        out_specs=pl.BlockSpec((tm, ch), lambda i: (i, 0)),
        scratch_shapes=[pltpu.VMEM((tm + SUBLANES, ch), F32)],
        compiler_params=_params("arbitrary"),
        name="conv_gate",
    )(h, h, h, conv_w)


def _rope(t, cosf, sinf):
    return t * cosf + pltpu.roll(t, ATT_HEAD_DIM // 2, 1) * sinf


def _attn_kernel(q_ref, kc_ref, kp_ref, vc_ref, vp_ref, cc_ref, sc_ref, cp_ref, sp_ref,
                 o_ref, q_s, k_s, v_s, acc_s, m_s, l_s):
    j = pl.program_id(2)
    bl = ATT_TOKENS
    blk = ATT_BLOCK
    cc = cc_ref[...]
    sc = sc_ref[...]
    q_s[...] = _rope(q_ref[...], cc, sc)
    k_s[bl:2 * bl, :] = _rope(kc_ref[...], cc, sc)
    k_s[0:bl, :] = _rope(kp_ref[...], cp_ref[...], sp_ref[...])
    v_s[bl:2 * bl, :] = vc_ref[...]
    v_s[0:bl, :] = vp_ref[...]

    qi = lax.broadcasted_iota(jnp.int32, (blk, 2 * blk), 0)
    kj = lax.broadcasted_iota(jnp.int32, (blk, 2 * blk), 1)
    dist = qi + blk - kj
    band = (dist >= 0) & (dist <= ATT_SPAN)
    band_first = band & (kj >= jnp.where(j > 0, 0, blk))
    scale = ATT_HEAD_DIM ** -0.5

    for branch, (_, d) in enumerate(DILATED_PAIRS):
        for g in range(bl // (blk * d)):
            for r in range(d):
                q_rows = pl.ds(g * blk * d + r, blk, stride=d) if d > 1 else pl.ds(g * blk, blk)
                k_start = bl + (g - 1) * blk * d + r
                k_rows = pl.ds(k_start, 2 * blk, stride=d) if d > 1 else pl.ds(k_start, 2 * blk)
                qs = q_s[q_rows, :].astype(BF16)
                ks = k_s[k_rows, :].astype(BF16)
                vs = v_s[k_rows, :].astype(BF16)
                s = lax.dot_general(qs, ks, _TRANS_B, preferred_element_type=F32) * scale
                s = jnp.where(band_first if g == 0 else band, s, NEG_INF)
                m_b = jnp.max(s, axis=-1, keepdims=True)
                e = jnp.exp(s - m_b)
                l_b = jnp.sum(e, axis=-1, keepdims=True)
                o_b = jnp.dot(e.astype(BF16), vs, preferred_element_type=F32)
                if branch == 0:
                    acc_s[q_rows, :] = o_b
                    m_s[q_rows, :] = jnp.broadcast_to(m_b, (blk, LANES))
                    l_s[q_rows, :] = jnp.broadcast_to(l_b, (blk, LANES))
                else:
                    m_old = m_s[q_rows, :]
                    m_new = jnp.maximum(m_old, m_b)
                    a_old = jnp.exp(m_old - m_new)
                    a_new = jnp.exp(m_b - m_new)
                    acc_s[q_rows, :] = a_old * acc_s[q_rows, :] + a_new * o_b
                    l_s[q_rows, :] = a_old * l_s[q_rows, :] + a_new * l_b
                    m_s[q_rows, :] = m_new

    o_ref[...] = (acc_s[...] / l_s[...]).astype(o_ref.dtype)


def _dilated_attention(h, cosf, sinf, batch, seq):
    t = h.shape[0]
    bl = ATT_TOKENS
    nj = seq // bl
    hd = ATT_HEAD_DIM
    q_col = 3 * CONV_CH // hd
    k_col = q_col + ATT_HEADS
    v_col = k_col + ATT_HEADS

    def cur(col):
        return lambda b, hh, j: (b * nj + j, col + hh)

    def prev(col):
        return lambda b, hh, j: (b * nj + jnp.maximum(j - 1, 0), col + hh)

    rows_cur = lambda b, hh, j: (b * nj + j, 0)
    rows_prev = lambda b, hh, j: (b * nj + jnp.maximum(j - 1, 0), 0)
    blk = (bl, hd)
    return pl.pallas_call(
        _attn_kernel,
        out_shape=jax.ShapeDtypeStruct((t, ATT_WIDTH), BF16),
        grid=(batch, ATT_HEADS, nj),
        in_specs=[pl.BlockSpec(blk, cur(q_col)),
                  pl.BlockSpec(blk, cur(k_col)), pl.BlockSpec(blk, prev(k_col)),
                  pl.BlockSpec(blk, cur(v_col)), pl.BlockSpec(blk, prev(v_col)),
                  pl.BlockSpec(blk, rows_cur), pl.BlockSpec(blk, rows_cur),
                  pl.BlockSpec(blk, rows_prev), pl.BlockSpec(blk, rows_prev)],
        out_specs=pl.BlockSpec(blk, lambda b, hh, j: (b * nj + j, hh)),
        scratch_shapes=[pltpu.VMEM((bl, hd), F32), pltpu.VMEM((2 * bl, hd), F32),
                        pltpu.VMEM((2 * bl, hd), F32), pltpu.VMEM((bl, hd), F32),
                        pltpu.VMEM((bl, LANES), F32), pltpu.VMEM((bl, LANES), F32)],
        compiler_params=_params("parallel", "parallel", "parallel"),
        name="dilated_attention",
    )(h, h, h, h, h, cosf, sinf, cosf, sinf)


def _log_sigmoid(x):
    return jnp.minimum(x, 0.0) - jnp.log(1.0 + jnp.exp(-jnp.abs(x)))


def _mlstm_kernel(q_ref, k_ref, v_ref, og_ref, g_ref, bias_ref, o_ref, c_s, n_s, m_s):
    nh, dk, dv, cl = MLSTM_HEADS, MLSTM_QK_DIM, MLSTM_V_DIM, MLSTM_CHUNK

    @pl.when(pl.program_id(1) == 0)
    def _():
        c_s[...] = jnp.zeros(c_s.shape, F32)
        n_s[...] = jnp.zeros(n_s.shape, F32)
        m_s[...] = jnp.zeros(m_s.shape, F32)

    gates = g_ref[...] + bias_ref[...]
    gates_t = gates.T
    row = lax.broadcasted_iota(jnp.int32, (cl, cl), 0)
    col = lax.broadcasted_iota(jnp.int32, (cl, cl), 1)
    causal = col <= row

    for hd in range(nh):
        ig_col = gates[:, hd:hd + 1]
        ig_row = gates_t[hd:hd + 1, :]
        lf_col = _log_sigmoid(gates[:, nh + hd:nh + hd + 1])
        lf_row = _log_sigmoid(gates_t[nh + hd:nh + hd + 1, :])
        bcum_col = jnp.sum(jnp.where(causal, lf_row, 0.0), axis=1, keepdims=True)
        bcum_row = jnp.sum(jnp.where(row <= col, lf_col, 0.0), axis=0, keepdims=True)
        btot = bcum_col[cl - 1:cl, :]

        qf = q_ref[:, hd * dk:(hd + 1) * dk]
        kf = k_ref[:, hd * dk:(hd + 1) * dk] * (dk ** -0.5)
        vf = v_ref[:, hd * dv:(hd + 1) * dv]
        qb = qf.astype(BF16)
        kb = kf.astype(BF16)
        c_prev = c_s[hd]
        n_prev = n_s[hd]
        m_prev = m_s[hd][:, 0:1]

        dmat = jnp.where(causal, bcum_col - bcum_row + ig_row, NEG_INF)
        g_inter = bcum_col + m_prev
        m_t = jnp.maximum(g_inter, jnp.max(dmat, axis=1, keepdims=True))
        qk = lax.dot_general(qb, kb, _TRANS_B, preferred_element_type=F32)
        sc = qk * jnp.exp(dmat - m_t)
        a_inter = jnp.exp(g_inter - m_t)
        num = (a_inter * lax.dot_general(qb, c_prev.astype(BF16), _TRANS_B,
                                         preferred_element_type=F32)
               + jnp.dot(sc.astype(BF16), vf.astype(BF16), preferred_element_type=F32))
        den = (a_inter * jnp.sum(qf * n_prev, axis=1, keepdims=True)
               + jnp.sum(sc, axis=1, keepdims=True))
        hh = num / jnp.maximum(jnp.abs(den), jnp.exp(-m_t))
        o_ref[:, hd * dv:(hd + 1) * dv] = (
            jax.nn.sigmoid(og_ref[:, hd * dv:(hd + 1) * dv]) * hh).astype(o_ref.dtype)

        w_loc = btot - bcum_col + ig_col
        m_loc = jnp.max(w_loc, axis=0, keepdims=True)
        a_loc = jnp.exp(w_loc - m_loc)
        c_loc = jnp.dot((vf * a_loc).T.astype(BF16), kb, preferred_element_type=F32)
        n_loc = jnp.sum(a_loc * kf, axis=0, keepdims=True)
        m_new = jnp.maximum(btot + m_prev, m_loc)
        a_old = jnp.exp(btot + m_prev - m_new)
        a_new = jnp.exp(m_loc - m_new)
        c_s[hd] = a_old * c_prev + a_new * c_loc
        n_s[hd] = a_old * n_prev + a_new * n_loc
        m_s[hd] = jnp.broadcast_to(m_new, (1, LANES))


def _mlstm(h, gates, gate_bias_row, batch, seq):
    t = h.shape[0]
    nh, dk, dv, cl = MLSTM_HEADS, MLSTM_QK_DIM, MLSTM_V_DIM, MLSTM_CHUNK
    nc = seq // cl
    rows = lambda col: (lambda b, c: (b * nc + c, col))
    return pl.pallas_call(
        _mlstm_kernel,
        out_shape=jax.ShapeDtypeStruct((t, nh * dv), BF16),
        grid=(batch, nc),
        in_specs=[pl.BlockSpec((cl, nh * dk), rows(0)),
                  pl.BlockSpec((cl, nh * dk), rows(1)),
                  pl.BlockSpec((cl, nh * dv), rows(1)),
                  pl.BlockSpec((cl, nh * dv), rows(2)),
                  pl.BlockSpec((cl, LANES), rows(0)),
                  pl.BlockSpec((1, LANES), lambda b, c: (0, 0))],
        out_specs=pl.BlockSpec((cl, nh * dv), rows(0)),
        scratch_shapes=[pltpu.VMEM((nh, dv, dk), F32), pltpu.VMEM((nh, 1, dk), F32),
                        pltpu.VMEM((nh, 1, LANES), F32)],
        compiler_params=_params("parallel", "arbitrary"),
        name="mlstm",
    )(h, h, h, h, gates, gate_bias_row)


_E1_GROUPS = PEER_NKEYS // SUBLANES
_BF16_PER_WORD = 2


def _pack_bf16(x):
    return pltpu.bitcast(x.astype(BF16), jnp.uint32)


def _unpack_bf16(words):
    return pltpu.bitcast(words, BF16)


def _bf16_rows(row, n_rows):
    tile_rows = SUBLANES * _BF16_PER_WORD
    return jnp.tile(jnp.broadcast_to(row, (tile_rows, row.shape[1])).astype(BF16),
                    (n_rows // tile_rows, 1))


def _top_values(s, k, with_rank=False):
    vals = []
    rank = jnp.full(s.shape, float(k), F32) if with_rank else None
    for it in range(k):
        mx = jnp.max(s, axis=0, keepdims=True)
        vals.append(mx)
        hit = s == mx
        if with_rank:
            rank = jnp.where(hit, float(it), rank)
        s = jnp.where(hit, NEG_INF, s)
    return vals, rank


def _peer_topk_kernel(q_ref, keys_ref, r2_ref, p2_ref, n_ref, p1_ref, s1_s, s2_s):
    kk = PEER_TOPK
    half = PEER_QDIM // 2
    tb = q_ref.shape[0]
    q = q_ref[...]
    s1_s[...] = lax.dot_general(keys_ref[0, 0], q[:, :half], _TRANS_B, preferred_element_type=F32)
    s2_s[...] = lax.dot_general(keys_ref[0, 1], q[:, half:], _TRANS_B, preferred_element_type=F32)

    def chunk(c, carry):
        lanes = pl.ds(pl.multiple_of(c * LANES, LANES), LANES)
        s1 = s1_s[:, lanes]
        s2 = s2_s[:, lanes]
        v1, _ = _top_values(s1, kk)
        v2, rank2 = _top_values(s2, kk, with_rank=True)
        v2_all = jnp.concatenate(v2, axis=0)
        slabs = [v1[0] + v2_all]
        slabs += [v1[a] + v2_all[0:kk // 2] for a in range(1, kk // 2)]
        slabs += [jnp.concatenate(v1[kk // 2:], axis=0) + v2[0]]
        cand = jnp.concatenate(slabs, axis=0)
        m = v1[0] + v2[0]
        den = jnp.zeros_like(m)
        thr = m
        for _ in range(kk):
            thr = jnp.max(cand, axis=0, keepdims=True)
            den = den + jnp.exp(thr - m)
            cand = jnp.where(cand == thr, NEG_INF, cand)
        r2_ref[0, :, lanes] = _pack_bf16(rank2)
        p2_ref[0, :, lanes] = _pack_bf16(jnp.exp(s2 - v2[0]) / den)
        n = jnp.zeros(s1.shape, F32)
        for b in range(kk):
            n = n + jnp.where(s1 + v2[b] >= thr, 1.0, 0.0)
        p1 = jnp.exp(s1 - v1[0])
        for g in range(_E1_GROUPS):
            rows = slice(g * SUBLANES, (g + 1) * SUBLANES)
            n_ref[g, 0, :, lanes] = n[rows]
            p1_ref[g, 0, :, lanes] = p1[rows]
        return carry

    lax.fori_loop(0, tb // LANES, chunk, 0)


def _peer_topk(qb, keys_b, tb=512):
    t = qb.shape[0]
    nh, nk = PEER_HEADS, PEER_NKEYS
    e2_spec = pl.BlockSpec((1, nk // _BF16_PER_WORD, tb), lambda i, h: (h, 0, i))
    e1_spec = pl.BlockSpec((_E1_GROUPS, 1, SUBLANES, tb), lambda i, h: (0, h, 0, i))
    return pl.pallas_call(
        _peer_topk_kernel,
        out_shape=(jax.ShapeDtypeStruct((nh, nk // _BF16_PER_WORD, t), jnp.uint32),
                   jax.ShapeDtypeStruct((nh, nk // _BF16_PER_WORD, t), jnp.uint32),
                   jax.ShapeDtypeStruct((_E1_GROUPS, nh, SUBLANES, t), F32),
                   jax.ShapeDtypeStruct((_E1_GROUPS, nh, SUBLANES, t), F32)),
        grid=(t // tb, nh),
        in_specs=[pl.BlockSpec((tb, PEER_QDIM), lambda i, h: (i, h)),
                  pl.BlockSpec((1, 2, nk, PEER_QDIM // 2), lambda i, h: (h, 0, 0, 0))],
        out_specs=(e2_spec, e2_spec, e1_spec, e1_spec),
        scratch_shapes=[pltpu.VMEM((nk, tb), F32), pltpu.VMEM((nk, tb), F32)],
        compiler_params=_params("parallel", "parallel"),
        name="peer_topk",
    )(qb, keys_b)


def _gelu_exact(x):
    return 0.5 * x * (1.0 + lax.erf(x * (2.0 ** -0.5)))


def _zero_after(v):
    bits = pltpu.bitcast(v, jnp.uint32)
    return pltpu.bitcast((bits >> 16) >> 16, F32)[0:1, :]


def _peer_tile_activations(hid_s, act_s, group, row0, r2_ref, p2_ref, n_ref, p1_ref, after):
    nh, nk = PEER_HEADS, PEER_NKEYS
    te, tb = hid_s.shape
    zero = jnp.zeros((nk, LANES), BF16)
    for c in range(tb // LANES):
        lanes = slice(c * LANES, (c + 1) * LANES)
        for le1 in range(te // nk):
            r = row0 + le1
            w = zero
            for h in range(nh):
                n_rows = _bf16_rows(n_ref[group, h, r:r + 1, lanes] + after, nk)
                p1_rows = _bf16_rows(p1_ref[group, h, r:r + 1, lanes], nk)
                keep = _unpack_bf16(r2_ref[h, :, lanes]) < n_rows
                w = w + jnp.where(keep, _unpack_bf16(p2_ref[h, :, lanes]) * p1_rows, zero)
            rows = slice(le1 * nk, (le1 + 1) * nk)
            act = w * _gelu_exact(hid_s[rows, lanes]).astype(BF16)
            act_s[rows, lanes] = act
            after = _zero_after(act[0:2 * SUBLANES, :])
    return after


def _peer_dense_kernel(x_ref, u_ref, vt_ref, r2_ref, p2_ref, n_ref, p1_ref, o_ref,
                       acc_s, hid0_s, hid1_s, act0_s, act1_s):
    te = hid0_s.shape[0]
    rows_per_tile = te // PEER_NKEYS
    s = pl.program_id(1)

    @pl.when(s == 0)
    def _():
        acc_s[...] = jnp.zeros(acc_s.shape, F32)
        hid1_s[...] = jnp.zeros(hid1_s.shape, F32)
        act0_s[...] = jnp.zeros(act0_s.shape, BF16)

    x = x_ref[...]
    group_prev = jnp.maximum(s - 1, 0)
    group_cur = jnp.minimum(s, _E1_GROUPS - 1)

    after = jnp.zeros((1, LANES), F32)
    hid0_s[...] = lax.dot_general(u_ref[0:te, :], x, _TRANS_B, preferred_element_type=F32)
    after = _peer_tile_activations(hid1_s, act1_s, group_prev, rows_per_tile,
                                   r2_ref, p2_ref, n_ref, p1_ref, after)
    acc_s[...] += jnp.dot(vt_ref[:, 0:te], act0_s[...], preferred_element_type=F32)

    hid1_s[...] = lax.dot_general(u_ref[te:2 * te, :], x, _TRANS_B, preferred_element_type=F32)
    _peer_tile_activations(hid0_s, act0_s, group_cur, 0, r2_ref, p2_ref, n_ref, p1_ref, after)
    acc_s[...] += jnp.dot(vt_ref[:, te:2 * te], act1_s[...], preferred_element_type=F32)

    @pl.when(s == pl.num_programs(1) - 1)
    def _():
        o_ref[...] = acc_s[...].T


def _peer_dense(xb, u_b, vt_b, r2, p2, n_rows, p1_rows, tb=512):
    t, dm = xb.shape
    nh, nk = PEER_HEADS, PEER_NKEYS
    ne = u_b.shape[0]
    te = SUBLANES * nk // 2
    n_pairs = ne // (2 * te)
    assert n_pairs == _E1_GROUPS
    e2_spec = pl.BlockSpec((nh, nk // _BF16_PER_WORD, tb), lambda i, s: (0, 0, i))
    e1_spec = pl.BlockSpec((_E1_GROUPS, nh, SUBLANES, tb), lambda i, s: (0, 0, 0, i))
    return pl.pallas_call(
        _peer_dense_kernel,
        out_shape=jax.ShapeDtypeStruct((t, dm), F32),
        grid=(t // tb, n_pairs + 1),
        in_specs=[pl.BlockSpec((tb, dm), lambda i, s: (i, 0)),
                  pl.BlockSpec((2 * te, dm), lambda i, s: (jnp.minimum(s, n_pairs - 1), 0)),
                  pl.BlockSpec((dm, 2 * te), lambda i, s: (0, jnp.maximum(s - 1, 0))),
                  e2_spec, e2_spec, e1_spec, e1_spec],
        out_specs=pl.BlockSpec((tb, dm), lambda i, s: (i, 0)),
        scratch_shapes=[pltpu.VMEM((dm, tb), F32),
                        pltpu.VMEM((te, tb), F32), pltpu.VMEM((te, tb), F32),
                        pltpu.VMEM((te, tb), BF16), pltpu.VMEM((te, tb), BF16)],
        compiler_params=_params("parallel", "arbitrary"),
        name="peer_dense",
    )(xb, u_b, vt_b, r2, p2, n_rows, p1_rows)


def _peer(xb, w_q_b, keys_b, u_b, vt_b):
    qb = _matmul(xb, w_q_b, BF16)
    r2, p2, n_rows, p1_rows = _peer_topk(qb, keys_b)
    return _peer_dense(xb, u_b, vt_b, r2, p2, n_rows, p1_rows)


def _rope_tables(positions):
    half = ATT_HEAD_DIM // 2
    inv_freq = jnp.power(ROPE_THETA, -2.0 * jnp.arange(half, dtype=F32) / ATT_HEAD_DIM)
    ang = positions.astype(F32).reshape(-1, 1) * inv_freq
    cos, sin = jnp.cos(ang), jnp.sin(ang)
    return jnp.concatenate([cos, cos], axis=-1), jnp.concatenate([-sin, sin], axis=-1)


def kernel(x, p, positions, even_w_in, even_conv_w, even_w_out, odd_w_in, odd_gate_bias,
           odd_w_out, peer_w_q, peer_sub_keys, peer_u, peer_v, ple_w_proj, ple_w_gate,
           ln_gain, ln_bias):
    batch, seq, dm = x.shape
    t = batch * seq
    assert dm == D_MODEL and seq % ATT_TOKENS == 0 and seq % MLSTM_CHUNK == 0
    xf = x.reshape(t, dm)
    xb = xf.astype(BF16)
    cosf, sinf = _rope_tables(positions)

    for layer in range(DEPTH):
        j = layer // 2
        if layer % 2 == 0:
            h = _matmul(xb, even_w_in[j].astype(BF16), F32)
            y_conv = _conv_gate(h, even_conv_w[j], seq)
            y_att = _dilated_attention(h, cosf, sinf, batch, seq)
            mixed = [y_conv, y_att]
            w_out = even_w_out[j]
        else:
            w_in = odd_w_in[j]
            h = _matmul(xb, w_in[:, :ODD_MAIN].astype(BF16), F32)
            n_gates = 2 * MLSTM_HEADS
            w_gates = jnp.pad(w_in[:, ODD_MAIN:], ((0, 0), (0, LANES - n_gates))).astype(BF16)
            gates = _matmul(xb, w_gates, F32)
            bias_row = jnp.pad(odd_gate_bias[j].reshape(1, n_gates), ((0, 0), (0, LANES - n_gates)))
            mixed = [_mlstm(h, gates, bias_row, batch, seq)]
            w_out = odd_w_out[j]
        xf, xb = _matmul_residual_ln(mixed, w_out.astype(BF16), xf,
                                     ln_gain[layer, 0], ln_bias[layer, 0])
        ffn = _peer(xb, peer_w_q[layer].astype(BF16), peer_sub_keys[layer].astype(BF16),
                    peer_u[layer].astype(BF16), peer_v[layer].astype(BF16).T)
        xf, xb = _ple_residual_ln(xf, xb, ffn, p[layer].reshape(t, PLE_DIM).astype(BF16),
                                  ple_w_gate[layer].astype(BF16), ple_w_proj[layer].astype(BF16),
                                  ln_gain[layer, 1], ln_bias[layer, 1])
    return xf.reshape(batch, seq, dm)
```

```python
import functools
import math

import jax
import jax.numpy as jnp
from jax import lax
from jax.experimental import pallas as pl
from jax.experimental.pallas import tpu as pltpu

D_MODEL = 2048
DEPTH = 4
CONV_CH = D_MODEL // 2
CONV_WIDTH = 3
ATT_HEADS = 8
ATT_HEAD_DIM = 128
ATT_WIDTH = ATT_HEADS * ATT_HEAD_DIM
DILATED_PAIRS = ((128, 1), (512, 4), (2048, 16))
ROPE_THETA = 10000.0
EVEN_IN = 3 * CONV_CH + 3 * ATT_WIDTH
MLSTM_HEADS = 8
MLSTM_QK_DIM = 128
MLSTM_V_DIM = D_MODEL // MLSTM_HEADS
MLSTM_CHUNK = 128
ODD_MAIN = 2 * MLSTM_HEADS * MLSTM_QK_DIM + 2 * MLSTM_HEADS * MLSTM_V_DIM
PEER_HEADS = 8
PEER_NKEYS = 128
PEER_QDIM = 256
PEER_TOPK = 16
PEER_EXPERTS = PEER_NKEYS * PEER_NKEYS
PLE_DIM = 256
ALPHA = (2.0 * DEPTH) ** 0.25
LN_EPS = 1e-5

LANES = 128
SUBLANES = 8
VMEM_LIMIT_BYTES = 56 * 1024 * 1024

ATT_BLOCK = 128
ATT_SPAN = 128
ATT_TOKENS = ATT_BLOCK * max(d for _, d in DILATED_PAIRS)

NEG_INF = float("-inf")
F32 = jnp.float32
BF16 = jnp.bfloat16

_TRANS_B = (((1,), (1,)), ((), ()))


def _params(*semantics):
    return pltpu.CompilerParams(dimension_semantics=semantics,
                                vmem_limit_bytes=VMEM_LIMIT_BYTES)


def _layer_norm_rows(y, gain, bias):
    mu = jnp.mean(y, axis=-1, keepdims=True)
    yc = y - mu
    var = jnp.mean(yc * yc, axis=-1, keepdims=True)
    return yc * lax.rsqrt(var + LN_EPS) * gain + bias


def _mm_kernel(a_ref, w_ref, o_ref):
    o_ref[...] = jnp.dot(a_ref[...], w_ref[...],
                         preferred_element_type=F32).astype(o_ref.dtype)


def _matmul(a, w, out_dtype, tm=512, tn=None):
    m, k = a.shape
    n = w.shape[1]
    tn = min(n, 2048) if tn is None else tn
    return pl.pallas_call(
        _mm_kernel,
        out_shape=jax.ShapeDtypeStruct((m, n), out_dtype),
        grid=(n // tn, m // tm),
        in_specs=[pl.BlockSpec((tm, k), lambda j, i: (i, 0)),
                  pl.BlockSpec((k, tn), lambda j, i: (0, j))],
        out_specs=pl.BlockSpec((tm, tn), lambda j, i: (i, j)),
        compiler_params=_params("parallel", "parallel"),
        name="matmul",
    )(a, w)


def _mm_ln_kernel(*refs, n_a):
    a_refs = refs[:n_a]
    w_ref, res_ref, g_ref, b_ref, o_ref, ob_ref = refs[n_a:]
    acc = None
    off = 0
    for a_ref in a_refs:
        k = a_ref.shape[1]
        part = jnp.dot(a_ref[...], w_ref[off:off + k, :], preferred_element_type=F32)
        acc = part if acc is None else acc + part
        off += k
    y = ALPHA * res_ref[...] + acc
    out = _layer_norm_rows(y, g_ref[...], b_ref[...])
    o_ref[...] = out
    ob_ref[...] = out.astype(BF16)


def _matmul_residual_ln(a_list, w, res, gain, bias, tm=256):
    m = res.shape[0]
    n = w.shape[1]
    in_specs = [pl.BlockSpec((tm, a.shape[1]), lambda i: (i, 0)) for a in a_list]
    in_specs += [pl.BlockSpec(w.shape, lambda i: (0, 0)),
                 pl.BlockSpec((tm, n), lambda i: (i, 0)),
                 pl.BlockSpec((1, n), lambda i: (0, 0)),
                 pl.BlockSpec((1, n), lambda i: (0, 0))]
    return pl.pallas_call(
        functools.partial(_mm_ln_kernel, n_a=len(a_list)),
        out_shape=(jax.ShapeDtypeStruct((m, n), F32), jax.ShapeDtypeStruct((m, n), BF16)),
        grid=(m // tm,),
        in_specs=in_specs,
        out_specs=(pl.BlockSpec((tm, n), lambda i: (i, 0)),
                   pl.BlockSpec((tm, n), lambda i: (i, 0))),
        compiler_params=_params("parallel"),
        name="matmul_residual_ln",
    )(*a_list, w, res, gain.reshape(1, n), bias.reshape(1, n))


def _ple_ln_kernel(x_ref, xb_ref, ffn_ref, p_ref, wg_ref, wp_ref, g_ref, b_ref, o_ref, ob_ref):
    gate = jax.nn.sigmoid(jnp.dot(xb_ref[...], wg_ref[...], preferred_element_type=F32))
    proj = jnp.dot(p_ref[...], wp_ref[...], preferred_element_type=F32)
    y = ALPHA * x_ref[...] + ffn_ref[...] + gate * proj
    out = _layer_norm_rows(y, g_ref[...], b_ref[...])
    o_ref[...] = out
    ob_ref[...] = out.astype(BF16)


def _ple_residual_ln(x, xb, ffn, p, wg, wp, gain, bias, tm=256):
    m, n = x.shape
    row = lambda i: (i, 0)
    const = lambda i: (0, 0)
    return pl.pallas_call(
        _ple_ln_kernel,
        out_shape=(jax.ShapeDtypeStruct((m, n), F32), jax.ShapeDtypeStruct((m, n), BF16)),
        grid=(m // tm,),
        in_specs=[pl.BlockSpec((tm, n), row), pl.BlockSpec((tm, n), row),
                  pl.BlockSpec((tm, n), row), pl.BlockSpec((tm, p.shape[1]), row),
                  pl.BlockSpec(wg.shape, const), pl.BlockSpec(wp.shape, const),
                  pl.BlockSpec((1, n), const), pl.BlockSpec((1, n), const)],
        out_specs=(pl.BlockSpec((tm, n), row), pl.BlockSpec((tm, n), row)),
        compiler_params=_params("parallel"),
        name="ple_residual_ln",
    )(x, xb, ffn, p, wg, wp, gain.reshape(1, n), bias.reshape(1, n))


def _conv_gate_kernel(gb_ref, gc_ref, xa_ref, w_ref, o_ref, buf, *, steps_per_seq):
    tm = gb_ref.shape[0]
    halo = SUBLANES

    @pl.when(pl.program_id(0) % steps_per_seq == 0)
    def _():
        buf[0:halo, :] = jnp.zeros((halo, buf.shape[1]), F32)

    u = gc_ref[...] * xa_ref[...]
    buf[halo:halo + tm, :] = u
    w = w_ref[...]
    conv = (w[0:1, :] * buf[halo - 2:halo - 2 + tm, :]
            + w[1:2, :] * buf[halo - 1:halo - 1 + tm, :]
            + w[2:3, :] * u)
    o_ref[...] = (gb_ref[...] * conv).astype(o_ref.dtype)
    buf[0:halo, :] = buf[tm:tm + halo, :]


def _conv_gate(h, conv_w, seq, tm=512):
    t = h.shape[0]
    ch = CONV_CH
    return pl.pallas_call(
        functools.partial(_conv_gate_kernel, steps_per_seq=seq // tm),
        out_shape=jax.ShapeDtypeStruct((t, ch), BF16),
        grid=(t // tm,),
        in_specs=[pl.BlockSpec((tm, ch), lambda i: (i, 0)),
                  pl.BlockSpec((tm, ch), lambda i: (i, 1)),
                  pl.BlockSpec((tm, ch), lambda i: (i, 2)),
                  pl.BlockSpec((CONV_WIDTH, ch), lambda i: (0, 0))],
        out_specs=pl.BlockSpec((tm, ch), lambda i: (i, 0)),
        scratch_shapes=[pltpu.VMEM((tm + SUBLANES, ch), F32)],
        compiler_params=_params("arbitrary"),
        name="conv_gate",
    )(h, h, h, conv_w)


def _rope(t, cosf, sinf):
    return t * cosf + pltpu.roll(t, ATT_HEAD_DIM // 2, 1) * sinf


def _attn_kernel(q_ref, kc_ref, kp_ref, vc_ref, vp_ref, cc_ref, sc_ref, cp_ref, sp_ref,
                 o_ref, q_s, k_s, v_s, acc_s, m_s, l_s):
    j = pl.program_id(2)
    bl = ATT_TOKENS
    blk = ATT_BLOCK
    cc = cc_ref[...]
    sc = sc_ref[...]
    q_s[...] = _rope(q_ref[...], cc, sc)
    k_s[bl:2 * bl, :] = _rope(kc_ref[...], cc, sc)
    k_s[0:bl, :] = _rope(kp_ref[...], cp_ref[...], sp_ref[...])
    v_s[bl:2 * bl, :] = vc_ref[...]
    v_s[0:bl, :] = vp_ref[...]

    qi = lax.broadcasted_iota(jnp.int32, (blk, 2 * blk), 0)
    kj = lax.broadcasted_iota(jnp.int32, (blk, 2 * blk), 1)
    dist = qi + blk - kj
    band = (dist >= 0) & (dist <= ATT_SPAN)
    band_first = band & (kj >= jnp.where(j > 0, 0, blk))
    scale = ATT_HEAD_DIM ** -0.5

    for branch, (_, d) in enumerate(DILATED_PAIRS):
        for g in range(bl // (blk * d)):
            for r in range(d):
                q_rows = pl.ds(g * blk * d + r, blk, stride=d) if d > 1 else pl.ds(g * blk, blk)
                k_start = bl + (g - 1) * blk * d + r
                k_rows = pl.ds(k_start, 2 * blk, stride=d) if d > 1 else pl.ds(k_start, 2 * blk)
                qs = q_s[q_rows, :].astype(BF16)
                ks = k_s[k_rows, :].astype(BF16)
                vs = v_s[k_rows, :].astype(BF16)
                s = lax.dot_general(qs, ks, _TRANS_B, preferred_element_type=F32) * scale
                s = jnp.where(band_first if g == 0 else band, s, NEG_INF)
                m_b = jnp.max(s, axis=-1, keepdims=True)
                e = jnp.exp(s - m_b)
                l_b = jnp.sum(e, axis=-1, keepdims=True)
                o_b = jnp.dot(e.astype(BF16), vs, preferred_element_type=F32)
                if branch == 0:
                    acc_s[q_rows, :] = o_b
                    m_s[q_rows, :] = jnp.broadcast_to(m_b, (blk, LANES))
                    l_s[q_rows, :] = jnp.broadcast_to(l_b, (blk, LANES))
                else:
                    m_old = m_s[q_rows, :]
                    m_new = jnp.maximum(m_old, m_b)
                    a_old = jnp.exp(m_old - m_new)
                    a_new = jnp.exp(m_b - m_new)
                    acc_s[q_rows, :] = a_old * acc_s[q_rows, :] + a_new * o_b
                    l_s[q_rows, :] = a_old * l_s[q_rows, :] + a_new * l_b
                    m_s[q_rows, :] = m_new

    o_ref[...] = (acc_s[...] / l_s[...]).astype(o_ref.dtype)


def _dilated_attention(h, cosf, sinf, batch, seq):
    t = h.shape[0]
    bl = ATT_TOKENS
    nj = seq // bl
    hd = ATT_HEAD_DIM
    q_col = 3 * CONV_CH // hd
    k_col = q_col + ATT_HEADS
    v_col = k_col + ATT_HEADS

    def cur(col):
        return lambda b, hh, j: (b * nj + j, col + hh)

    def prev(col):
        return lambda b, hh, j: (b * nj + jnp.maximum(j - 1, 0), col + hh)

    rows_cur = lambda b, hh, j: (b * nj + j, 0)
    rows_prev = lambda b, hh, j: (b * nj + jnp.maximum(j - 1, 0), 0)
    blk = (bl, hd)
    return pl.pallas_call(
        _attn_kernel,
        out_shape=jax.ShapeDtypeStruct((t, ATT_WIDTH), BF16),
        grid=(batch, ATT_HEADS, nj),
        in_specs=[pl.BlockSpec(blk, cur(q_col)),
                  pl.BlockSpec(blk, cur(k_col)), pl.BlockSpec(blk, prev(k_col)),
                  pl.BlockSpec(blk, cur(v_col)), pl.BlockSpec(blk, prev(v_col)),
                  pl.BlockSpec(blk, rows_cur), pl.BlockSpec(blk, rows_cur),
                  pl.BlockSpec(blk, rows_prev), pl.BlockSpec(blk, rows_prev)],
        out_specs=pl.BlockSpec(blk, lambda b, hh, j: (b * nj + j, hh)),
        scratch_shapes=[pltpu.VMEM((bl, hd), F32), pltpu.VMEM((2 * bl, hd), F32),
                        pltpu.VMEM((2 * bl, hd), F32), pltpu.VMEM((bl, hd), F32),
                        pltpu.VMEM((bl, LANES), F32), pltpu.VMEM((bl, LANES), F32)],
        compiler_params=_params("parallel", "parallel", "parallel"),
        name="dilated_attention",
    )(h, h, h, h, h, cosf, sinf, cosf, sinf)


def _log_sigmoid(x):
    return jnp.minimum(x, 0.0) - jnp.log(1.0 + jnp.exp(-jnp.abs(x)))


def _mlstm_kernel(q_ref, k_ref, v_ref, og_ref, g_ref, bias_ref, o_ref, c_s, n_s, m_s):
    nh, dk, dv, cl = MLSTM_HEADS, MLSTM_QK_DIM, MLSTM_V_DIM, MLSTM_CHUNK

    @pl.when(pl.program_id(1) == 0)
    def _():
        c_s[...] = jnp.zeros(c_s.shape, F32)
        n_s[...] = jnp.zeros(n_s.shape, F32)
        m_s[...] = jnp.zeros(m_s.shape, F32)

    gates = g_ref[...] + bias_ref[...]
    gates_t = gates.T
    row = lax.broadcasted_iota(jnp.int32, (cl, cl), 0)
    col = lax.broadcasted_iota(jnp.int32, (cl, cl), 1)
    causal = col <= row

    for hd in range(nh):
        ig_col = gates[:, hd:hd + 1]
        ig_row = gates_t[hd:hd + 1, :]
        lf_col = _log_sigmoid(gates[:, nh + hd:nh + hd + 1])
        lf_row = _log_sigmoid(gates_t[nh + hd:nh + hd + 1, :])
        bcum_col = jnp.sum(jnp.where(causal, lf_row, 0.0), axis=1, keepdims=True)
        bcum_row = jnp.sum(jnp.where(row <= col, lf_col, 0.0), axis=0, keepdims=True)
        btot = bcum_col[cl - 1:cl, :]

        qf = q_ref[:, hd * dk:(hd + 1) * dk]
        kf = k_ref[:, hd * dk:(hd + 1) * dk] * (dk ** -0.5)
        vf = v_ref[:, hd * dv:(hd + 1) * dv]
        qb = qf.astype(BF16)
        kb = kf.astype(BF16)
        c_prev = c_s[hd]
        n_prev = n_s[hd]
        m_prev = m_s[hd][:, 0:1]

        dmat = jnp.where(causal, bcum_col - bcum_row + ig_row, NEG_INF)
        g_inter = bcum_col + m_prev
        m_t = jnp.maximum(g_inter, jnp.max(dmat, axis=1, keepdims=True))
        qk = lax.dot_general(qb, kb, _TRANS_B, preferred_element_type=F32)
        sc = qk * jnp.exp(dmat - m_t)
        a_inter = jnp.exp(g_inter - m_t)
        num = (a_inter * lax.dot_general(qb, c_prev.astype(BF16), _TRANS_B,
                                         preferred_element_type=F32)
               + jnp.dot(sc.astype(BF16), vf.astype(BF16), preferred_element_type=F32))
        den = (a_inter * jnp.sum(qf * n_prev, axis=1, keepdims=True)
               + jnp.sum(sc, axis=1, keepdims=True))
        hh = num / jnp.maximum(jnp.abs(den), jnp.exp(-m_t))
        o_ref[:, hd * dv:(hd + 1) * dv] = (
            jax.nn.sigmoid(og_ref[:, hd * dv:(hd + 1) * dv]) * hh).astype(o_ref.dtype)

        w_loc = btot - bcum_col + ig_col
        m_loc = jnp.max(w_loc, axis=0, keepdims=True)
        a_loc = jnp.exp(w_loc - m_loc)
        c_loc = jnp.dot((vf * a_loc).T.astype(BF16), kb, preferred_element_type=F32)
        n_loc = jnp.sum(a_loc * kf, axis=0, keepdims=True)
        m_new = jnp.maximum(btot + m_prev, m_loc)
        a_old = jnp.exp(btot + m_prev - m_new)
        a_new = jnp.exp(m_loc - m_new)
        c_s[hd] = a_old * c_prev + a_new * c_loc
        n_s[hd] = a_old * n_prev + a_new * n_loc
        m_s[hd] = jnp.broadcast_to(m_new, (1, LANES))


def _mlstm(h, gates, gate_bias_row, batch, seq):
    t = h.shape[0]
    nh, dk, dv, cl = MLSTM_HEADS, MLSTM_QK_DIM, MLSTM_V_DIM, MLSTM_CHUNK
    nc = seq // cl
    rows = lambda col: (lambda b, c: (b * nc + c, col))
    return pl.pallas_call(
        _mlstm_kernel,
        out_shape=jax.ShapeDtypeStruct((t, nh * dv), BF16),
        grid=(batch, nc),
        in_specs=[pl.BlockSpec((cl, nh * dk), rows(0)),
                  pl.BlockSpec((cl, nh * dk), rows(1)),
                  pl.BlockSpec((cl, nh * dv), rows(1)),
                  pl.BlockSpec((cl, nh * dv), rows(2)),
                  pl.BlockSpec((cl, LANES), rows(0)),
                  pl.BlockSpec((1, LANES), lambda b, c: (0, 0))],
        out_specs=pl.BlockSpec((cl, nh * dv), rows(0)),
        scratch_shapes=[pltpu.VMEM((nh, dv, dk), F32), pltpu.VMEM((nh, 1, dk), F32),
                        pltpu.VMEM((nh, 1, LANES), F32)],
        compiler_params=_params("parallel", "arbitrary"),
        name="mlstm",
    )(h, h, h, h, gates, gate_bias_row)


_E1_GROUPS = PEER_NKEYS // SUBLANES
_BF16_PER_WORD = 2


def _pack_bf16(x):
    return pltpu.bitcast(x.astype(BF16), jnp.uint32)


def _unpack_bf16(words):
    return pltpu.bitcast(words, BF16)


def _bf16_rows(row, n_rows):
    tile_rows = SUBLANES * _BF16_PER_WORD
    return jnp.tile(jnp.broadcast_to(row, (tile_rows, row.shape[1])).astype(BF16),
                    (n_rows // tile_rows, 1))


def _top_values(s, k, with_rank=False):
    vals = []
    rank = jnp.full(s.shape, float(k), F32) if with_rank else None
    for it in range(k):
        mx = jnp.max(s, axis=0, keepdims=True)
        vals.append(mx)
        hit = s == mx
        if with_rank:
            rank = jnp.where(hit, float(it), rank)
        s = jnp.where(hit, NEG_INF, s)
    return vals, rank


def _peer_topk_kernel(q_ref, keys_ref, r2_ref, p2_ref, n_ref, p1_ref, s1_s, s2_s):
    kk = PEER_TOPK
    half = PEER_QDIM // 2
    tb = q_ref.shape[0]
    q = q_ref[...]
    s1_s[...] = lax.dot_general(keys_ref[0, 0], q[:, :half], _TRANS_B, preferred_element_type=F32)
    s2_s[...] = lax.dot_general(keys_ref[0, 1], q[:, half:], _TRANS_B, preferred_element_type=F32)

    def chunk(c):
        lanes = pl.ds(pl.multiple_of(c * LANES, LANES), LANES)
        s1 = s1_s[:, lanes]
        s2 = s2_s[:, lanes]
        v1, _ = _top_values(s1, kk)
        v2, rank2 = _top_values(s2, kk, with_rank=True)
        v2_all = jnp.concatenate(v2, axis=0)
        slabs = [v1[0] + v2_all]
        slabs += [v1[a] + v2_all[0:kk // 2] for a in range(1, kk // 2)]
        slabs += [jnp.concatenate(v1[kk // 2:], axis=0) + v2[0]]
        cand = jnp.concatenate(slabs, axis=0)
        m = v1[0] + v2[0]
        den = jnp.zeros_like(m)
        thr = m
        for _ in range(kk):
            thr = jnp.max(cand, axis=0, keepdims=True)
            den = den + jnp.exp(thr - m)
            cand = jnp.where(cand == thr, NEG_INF, cand)
        r2_ref[0, :, lanes] = _pack_bf16(rank2)
        p2_ref[0, :, lanes] = _pack_bf16(jnp.exp(s2 - v2[0]) / den)
        n = jnp.zeros(s1.shape, F32)
        for b in range(kk):
            n = n + jnp.where(s1 + v2[b] >= thr, 1.0, 0.0)
        p1 = jnp.exp(s1 - v1[0])
        for g in range(_E1_GROUPS):
            rows = slice(g * SUBLANES, (g + 1) * SUBLANES)
            n_ref[g, 0, :, lanes] = n[rows]
            p1_ref[g, 0, :, lanes] = p1[rows]

    def chunk_pair(i, carry):
        chunk(2 * i)
        chunk(2 * i + 1)
        return carry

    lax.fori_loop(0, tb // (2 * LANES), chunk_pair, 0)


def _peer_topk(qb, keys_b, tb=512):
    t = qb.shape[0]
    nh, nk = PEER_HEADS, PEER_NKEYS
    e2_spec = pl.BlockSpec((1, nk // _BF16_PER_WORD, tb), lambda i, h: (h, 0, i))
    e1_spec = pl.BlockSpec((_E1_GROUPS, 1, SUBLANES, tb), lambda i, h: (0, h, 0, i))
    return pl.pallas_call(
        _peer_topk_kernel,
        out_shape=(jax.ShapeDtypeStruct((nh, nk // _BF16_PER_WORD, t), jnp.uint32),
                   jax.ShapeDtypeStruct((nh, nk // _BF16_PER_WORD, t), jnp.uint32),
                   jax.ShapeDtypeStruct((_E1_GROUPS, nh, SUBLANES, t), F32),
                   jax.ShapeDtypeStruct((_E1_GROUPS, nh, SUBLANES, t), F32)),
        grid=(t // tb, nh),
        in_specs=[pl.BlockSpec((tb, PEER_QDIM), lambda i, h: (i, h)),
                  pl.BlockSpec((1, 2, nk, PEER_QDIM // 2), lambda i, h: (h, 0, 0, 0))],
        out_specs=(e2_spec, e2_spec, e1_spec, e1_spec),
        scratch_shapes=[pltpu.VMEM((nk, tb), F32), pltpu.VMEM((nk, tb), F32)],
        compiler_params=_params("parallel", "parallel"),
        name="peer_topk",
    )(qb, keys_b)


def _gelu_exact(x):
    return 0.5 * x * (1.0 + lax.erf(x * (2.0 ** -0.5)))


def _zero_after(v):
    bits = pltpu.bitcast(v, jnp.uint32)
    return pltpu.bitcast((bits >> 16) >> 16, F32)[0:1, :]


def _peer_tile_activations(hid_s, act_s, group, row0, r2_ref, p2_ref, n_ref, p1_ref, after):
    nh, nk = PEER_HEADS, PEER_NKEYS
    te, tb = hid_s.shape
    zero = jnp.zeros((nk, LANES), BF16)
    for c in range(tb // LANES):
        lanes = slice(c * LANES, (c + 1) * LANES)
        for le1 in range(te // nk):
            r = row0 + le1
            w = zero
            for h in range(nh):
                n_rows = _bf16_rows(n_ref[group, h, r:r + 1, lanes] + after, nk)
                p1_rows = _bf16_rows(p1_ref[group, h, r:r + 1, lanes], nk)
                keep = _unpack_bf16(r2_ref[h, :, lanes]) < n_rows
                w = w + jnp.where(keep, _unpack_bf16(p2_ref[h, :, lanes]) * p1_rows, zero)
            rows = slice(le1 * nk, (le1 + 1) * nk)
            act = w * _gelu_exact(hid_s[rows, lanes]).astype(BF16)
            act_s[rows, lanes] = act
            after = _zero_after(act[0:2 * SUBLANES, :])
    return after


def _peer_dense_kernel(x_ref, u_ref, vt_ref, r2_ref, p2_ref, n_ref, p1_ref, o_ref,
                       acc_s, hid0_s, hid1_s, act0_s, act1_s):
    te = hid0_s.shape[0]
    rows_per_tile = te // PEER_NKEYS
    s = pl.program_id(1)

    @pl.when(s == 0)
    def _():
        acc_s[...] = jnp.zeros(acc_s.shape, F32)
        hid1_s[...] = jnp.zeros(hid1_s.shape, F32)
        act0_s[...] = jnp.zeros(act0_s.shape, BF16)

    x = x_ref[...]
    group_prev = jnp.maximum(s - 1, 0)
    group_cur = jnp.minimum(s, _E1_GROUPS - 1)

    after = jnp.zeros((1, LANES), F32)
    after = _peer_tile_activations(hid1_s, act1_s, group_prev, rows_per_tile,
                                   r2_ref, p2_ref, n_ref, p1_ref, after)
    acc_s[...] += jnp.dot(vt_ref[:, 0:te], act0_s[...], preferred_element_type=F32)
    hid0_s[...] = lax.dot_general(u_ref[0:te, :], x, _TRANS_B, preferred_element_type=F32)

    hid1_s[...] = lax.dot_general(u_ref[te:2 * te, :], x, _TRANS_B, preferred_element_type=F32)
    _peer_tile_activations(hid0_s, act0_s, group_cur, 0, r2_ref, p2_ref, n_ref, p1_ref, after)
    acc_s[...] += jnp.dot(vt_ref[:, te:2 * te], act1_s[...], preferred_element_type=F32)

    @pl.when(s == pl.num_programs(1) - 1)
    def _():
        o_ref[...] = acc_s[...].T


def _peer_dense(xb, u_b, v_b, r2, p2, n_rows, p1_rows, tb=512):
    t, dm = xb.shape
    nh, nk = PEER_HEADS, PEER_NKEYS
    ne = u_b.shape[0]
    te = SUBLANES * nk // 2
    n_pairs = ne // (2 * te)
    assert n_pairs == _E1_GROUPS
    vt_b = v_b.reshape(n_pairs, 2 * te, dm).transpose(0, 2, 1)
    e2_spec = pl.BlockSpec((nh, nk // _BF16_PER_WORD, tb), lambda i, s: (0, 0, i))
    e1_spec = pl.BlockSpec((_E1_GROUPS, nh, SUBLANES, tb), lambda i, s: (0, 0, 0, i))
    return pl.pallas_call(
        _peer_dense_kernel,
        out_shape=jax.ShapeDtypeStruct((t, dm), F32),
        grid=(t // tb, n_pairs + 1),
        in_specs=[pl.BlockSpec((tb, dm), lambda i, s: (i, 0)),
                  pl.BlockSpec((2 * te, dm), lambda i, s: (jnp.minimum(s, n_pairs - 1), 0)),
                  pl.BlockSpec((None, dm, 2 * te), lambda i, s: (jnp.maximum(s - 1, 0), 0, 0)),
                  e2_spec, e2_spec, e1_spec, e1_spec],
        out_specs=pl.BlockSpec((tb, dm), lambda i, s: (i, 0)),
        scratch_shapes=[pltpu.VMEM((dm, tb), F32),
                        pltpu.VMEM((te, tb), F32), pltpu.VMEM((te, tb), F32),
                        pltpu.VMEM((te, tb), BF16), pltpu.VMEM((te, tb), BF16)],
        compiler_params=_params("parallel", "arbitrary"),
        name="peer_dense",
    )(xb, u_b, vt_b, r2, p2, n_rows, p1_rows)


def _peer(xb, w_q_b, keys_b, u_b, v_b):
    qb = _matmul(xb, w_q_b, BF16)
    r2, p2, n_rows, p1_rows = _peer_topk(qb, keys_b)
    return _peer_dense(xb, u_b, v_b, r2, p2, n_rows, p1_rows)


def _rope_tables(positions):
    half = ATT_HEAD_DIM // 2
    inv_freq = jnp.power(ROPE_THETA, -2.0 * jnp.arange(half, dtype=F32) / ATT_HEAD_DIM)
    ang = positions.astype(F32).reshape(-1, 1) * inv_freq
    cos, sin = jnp.cos(ang), jnp.sin(ang)
    return jnp.concatenate([cos, cos], axis=-1), jnp.concatenate([-sin, sin], axis=-1)


def kernel(x, p, positions, even_w_in, even_conv_w, even_w_out, odd_w_in, odd_gate_bias,
           odd_w_out, peer_w_q, peer_sub_keys, peer_u, peer_v, ple_w_proj, ple_w_gate,
           ln_gain, ln_bias):
    batch, seq, dm = x.shape
    t = batch * seq
    assert dm == D_MODEL and seq % ATT_TOKENS == 0 and seq % MLSTM_CHUNK == 0
    xf = x.reshape(t, dm)
    xb = xf.astype(BF16)
    cosf, sinf = _rope_tables(positions)

    for layer in range(DEPTH):
        j = layer // 2
        if layer % 2 == 0:
            h = _matmul(xb, even_w_in[j].astype(BF16), F32)
            y_conv = _conv_gate(h, even_conv_w[j], seq)
            y_att = _dilated_attention(h, cosf, sinf, batch, seq)
            mixed = [y_conv, y_att]
            w_out = even_w_out[j]
        else:
            w_in = odd_w_in[j]
            h = _matmul(xb, w_in[:, :ODD_MAIN].astype(BF16), F32)
            n_gates = 2 * MLSTM_HEADS
            w_gates = jnp.pad(w_in[:, ODD_MAIN:], ((0, 0), (0, LANES - n_gates))).astype(BF16)
            gates = _matmul(xb, w_gates, F32)
            bias_row = jnp.pad(odd_gate_bias[j].reshape(1, n_gates), ((0, 0), (0, LANES - n_gates)))
            mixed = [_mlstm(h, gates, bias_row, batch, seq)]
            w_out = odd_w_out[j]
        xf, xb = _matmul_residual_ln(mixed, w_out.astype(BF16), xf,
                                     ln_gain[layer, 0], ln_bias[layer, 0])
        ffn = _peer(xb, peer_w_q[layer].astype(BF16), peer_sub_keys[layer].astype(BF16),
                    peer_u[layer].astype(BF16), peer_v[layer].astype(BF16))
        xf, xb = _ple_residual_ln(xf, xb, ffn, p[layer].reshape(t, PLE_DIM).astype(BF16),
                                  ple_w_gate[layer].astype(BF16), ple_w_proj[layer].astype(BF16),
                                  ln_gain[layer, 1], ln_bias[layer, 1])
    return xf.reshape(batch, seq, dm)
```

```python
import functools
import math

import jax
import jax.numpy as jnp
from jax import lax
from jax.experimental import pallas as pl
from jax.experimental.pallas import tpu as pltpu

D_MODEL = 2048
DEPTH = 4
CONV_CH = D_MODEL // 2
CONV_WIDTH = 3
ATT_HEADS = 8
ATT_HEAD_DIM = 128
ATT_WIDTH = ATT_HEADS * ATT_HEAD_DIM
DILATED_PAIRS = ((128, 1), (512, 4), (2048, 16))
ROPE_THETA = 10000.0
EVEN_IN = 3 * CONV_CH + 3 * ATT_WIDTH
MLSTM_HEADS = 8
MLSTM_QK_DIM = 128
MLSTM_V_DIM = D_MODEL // MLSTM_HEADS
MLSTM_CHUNK = 128
ODD_MAIN = 2 * MLSTM_HEADS * MLSTM_QK_DIM + 2 * MLSTM_HEADS * MLSTM_V_DIM
PEER_HEADS = 8
PEER_NKEYS = 128
PEER_QDIM = 256
PEER_TOPK = 16
PEER_EXPERTS = PEER_NKEYS * PEER_NKEYS
PLE_DIM = 256
ALPHA = (2.0 * DEPTH) ** 0.25
LN_EPS = 1e-5

LANES = 128
SUBLANES = 8
VMEM_LIMIT_BYTES = 56 * 1024 * 1024

ATT_BLOCK = 128
ATT_SPAN = 128
ATT_TOKENS = ATT_BLOCK * max(d for _, d in DILATED_PAIRS)

NEG_INF = float("-inf")
F32 = jnp.float32
BF16 = jnp.bfloat16

_TRANS_B = (((1,), (1,)), ((), ()))


def _params(*semantics):
    return pltpu.CompilerParams(dimension_semantics=semantics,
                                vmem_limit_bytes=VMEM_LIMIT_BYTES)


def _layer_norm_rows(y, gain, bias):
    mu = jnp.mean(y, axis=-1, keepdims=True)
    yc = y - mu
    var = jnp.mean(yc * yc, axis=-1, keepdims=True)
    return yc * lax.rsqrt(var + LN_EPS) * gain + bias


def _mm_kernel(a_ref, w_ref, o_ref):
    o_ref[...] = jnp.dot(a_ref[...], w_ref[...],
                         preferred_element_type=F32).astype(o_ref.dtype)


def _matmul(a, w, out_dtype, tm=512, tn=None):
    m, k = a.shape
    n = w.shape[1]
    tn = min(n, 2048) if tn is None else tn
    return pl.pallas_call(
        _mm_kernel,
        out_shape=jax.ShapeDtypeStruct((m, n), out_dtype),
        grid=(n // tn, m // tm),
        in_specs=[pl.BlockSpec((tm, k), lambda j, i: (i, 0)),
                  pl.BlockSpec((k, tn), lambda j, i: (0, j))],
        out_specs=pl.BlockSpec((tm, tn), lambda j, i: (i, j)),
        compiler_params=_params("parallel", "parallel"),
        name="matmul",
    )(a, w)


def _mm_ln_kernel(*refs, n_a):
    a_refs = refs[:n_a]
    w_ref, res_ref, g_ref, b_ref, o_ref, ob_ref = refs[n_a:]
    acc = None
    off = 0
    for a_ref in a_refs:
        k = a_ref.shape[1]
        part = jnp.dot(a_ref[...], w_ref[off:off + k, :], preferred_element_type=F32)
        acc = part if acc is None else acc + part
        off += k
    y = ALPHA * res_ref[...] + acc
    out = _layer_norm_rows(y, g_ref[...], b_ref[...])
    o_ref[...] = out
    ob_ref[...] = out.astype(BF16)


def _matmul_residual_ln(a_list, w, res, gain, bias, tm=256):
    m = res.shape[0]
    n = w.shape[1]
    in_specs = [pl.BlockSpec((tm, a.shape[1]), lambda i: (i, 0)) for a in a_list]
    in_specs += [pl.BlockSpec(w.shape, lambda i: (0, 0)),
                 pl.BlockSpec((tm, n), lambda i: (i, 0)),
                 pl.BlockSpec((1, n), lambda i: (0, 0)),
                 pl.BlockSpec((1, n), lambda i: (0, 0))]
    return pl.pallas_call(
        functools.partial(_mm_ln_kernel, n_a=len(a_list)),
        out_shape=(jax.ShapeDtypeStruct((m, n), F32), jax.ShapeDtypeStruct((m, n), BF16)),
        grid=(m // tm,),
        in_specs=in_specs,
        out_specs=(pl.BlockSpec((tm, n), lambda i: (i, 0)),
                   pl.BlockSpec((tm, n), lambda i: (i, 0))),
        compiler_params=_params("parallel"),
        name="matmul_residual_ln",
    )(*a_list, w, res, gain.reshape(1, n), bias.reshape(1, n))


def _ple_ln_kernel(x_ref, xb_ref, ffn_ref, p_ref, wg_ref, wp_ref, g_ref, b_ref, o_ref, ob_ref):
    gate = jax.nn.sigmoid(jnp.dot(xb_ref[...], wg_ref[...], preferred_element_type=F32))
    proj = jnp.dot(p_ref[...], wp_ref[...], preferred_element_type=F32)
    y = ALPHA * x_ref[...] + ffn_ref[...] + gate * proj
    out = _layer_norm_rows(y, g_ref[...], b_ref[...])
    o_ref[...] = out
    ob_ref[...] = out.astype(BF16)


def _ple_residual_ln(x, xb, ffn, p, wg, wp, gain, bias, tm=256):
    m, n = x.shape
    row = lambda i: (i, 0)
    const = lambda i: (0, 0)
    return pl.pallas_call(
        _ple_ln_kernel,
        out_shape=(jax.ShapeDtypeStruct((m, n), F32), jax.ShapeDtypeStruct((m, n), BF16)),
        grid=(m // tm,),
        in_specs=[pl.BlockSpec((tm, n), row), pl.BlockSpec((tm, n), row),
                  pl.BlockSpec((tm, n), row), pl.BlockSpec((tm, p.shape[1]), row),
                  pl.BlockSpec(wg.shape, const), pl.BlockSpec(wp.shape, const),
                  pl.BlockSpec((1, n), const), pl.BlockSpec((1, n), const)],
        out_specs=(pl.BlockSpec((tm, n), row), pl.BlockSpec((tm, n), row)),
        compiler_params=_params("parallel"),
        name="ple_residual_ln",
    )(x, xb, ffn, p, wg, wp, gain.reshape(1, n), bias.reshape(1, n))


def _conv_gate_kernel(gb_ref, gc_ref, xa_ref, w_ref, o_ref, buf, *, steps_per_seq):
    tm = gb_ref.shape[0]
    halo = SUBLANES

    @pl.when(pl.program_id(0) % steps_per_seq == 0)
    def _():
        buf[0:halo, :] = jnp.zeros((halo, buf.shape[1]), F32)

    u = gc_ref[...] * xa_ref[...]
    buf[halo:halo + tm, :] = u
    w = w_ref[...]
    conv = (w[0:1, :] * buf[halo - 2:halo - 2 + tm, :]
            + w[1:2, :] * buf[halo - 1:halo - 1 + tm, :]
            + w[2:3, :] * u)
    o_ref[...] = (gb_ref[...] * conv).astype(o_ref.dtype)
    buf[0:halo, :] = buf[tm:tm + halo, :]


def _conv_gate(h, conv_w, seq, tm=512):
    t = h.shape[0]
    ch = CONV_CH
    return pl.pallas_call(
        functools.partial(_conv_gate_kernel, steps_per_seq=seq // tm),
        out_shape=jax.ShapeDtypeStruct((t, ch), BF16),
        grid=(t // tm,),
        in_specs=[pl.BlockSpec((tm, ch), lambda i: (i, 0)),
                  pl.BlockSpec((tm, ch), lambda i: (i, 1)),
                  pl.BlockSpec((tm, ch), lambda i: (i, 2)),
                  pl.BlockSpec((CONV_WIDTH, ch), lambda i: (0, 0))],
        out_specs=pl.BlockSpec((tm, ch), lambda i: (i, 0)),
        scratch_shapes=[pltpu.VMEM((tm + SUBLANES, ch), F32)],
        compiler_params=_params("arbitrary"),
        name="conv_gate",
    )(h, h, h, conv_w)


def _rope(t, cosf, sinf):
    return t * cosf + pltpu.roll(t, ATT_HEAD_DIM // 2, 1) * sinf


def _attn_kernel(q_ref, kc_ref, kp_ref, vc_ref, vp_ref, cc_ref, sc_ref, cp_ref, sp_ref,
                 o_ref, q_s, k_s, v_s, acc_s, m_s, l_s):
    j = pl.program_id(2)
    bl = ATT_TOKENS
    blk = ATT_BLOCK
    cc = cc_ref[...]
    sc = sc_ref[...]
    q_s[...] = _rope(q_ref[...], cc, sc)
    k_s[bl:2 * bl, :] = _rope(kc_ref[...], cc, sc)
    k_s[0:bl, :] = _rope(kp_ref[...], cp_ref[...], sp_ref[...])
    v_s[bl:2 * bl, :] = vc_ref[...]
    v_s[0:bl, :] = vp_ref[...]

    qi = lax.broadcasted_iota(jnp.int32, (blk, 2 * blk), 0)
    kj = lax.broadcasted_iota(jnp.int32, (blk, 2 * blk), 1)
    dist = qi + blk - kj
    band = (dist >= 0) & (dist <= ATT_SPAN)
    band_first = band & (kj >= jnp.where(j > 0, 0, blk))
    scale = ATT_HEAD_DIM ** -0.5

    for branch, (_, d) in enumerate(DILATED_PAIRS):
        for g in range(bl // (blk * d)):
            for r in range(d):
                q_rows = pl.ds(g * blk * d + r, blk, stride=d) if d > 1 else pl.ds(g * blk, blk)
                k_start = bl + (g - 1) * blk * d + r
                k_rows = pl.ds(k_start, 2 * blk, stride=d) if d > 1 else pl.ds(k_start, 2 * blk)
                qs = q_s[q_rows, :].astype(BF16)
                ks = k_s[k_rows, :].astype(BF16)
                vs = v_s[k_rows, :].astype(BF16)
                s = lax.dot_general(qs, ks, _TRANS_B, preferred_element_type=F32) * scale
                s = jnp.where(band_first if g == 0 else band, s, NEG_INF)
                m_b = jnp.max(s, axis=-1, keepdims=True)
                e = jnp.exp(s - m_b)
                l_b = jnp.sum(e, axis=-1, keepdims=True)
                o_b = jnp.dot(e.astype(BF16), vs, preferred_element_type=F32)
                if branch == 0:
                    acc_s[q_rows, :] = o_b
                    m_s[q_rows, :] = jnp.broadcast_to(m_b, (blk, LANES))
                    l_s[q_rows, :] = jnp.broadcast_to(l_b, (blk, LANES))
                else:
                    m_old = m_s[q_rows, :]
                    m_new = jnp.maximum(m_old, m_b)
                    a_old = jnp.exp(m_old - m_new)
                    a_new = jnp.exp(m_b - m_new)
                    acc_s[q_rows, :] = a_old * acc_s[q_rows, :] + a_new * o_b
                    l_s[q_rows, :] = a_old * l_s[q_rows, :] + a_new * l_b
                    m_s[q_rows, :] = m_new

    o_ref[...] = (acc_s[...] / l_s[...]).astype(o_ref.dtype)


def _dilated_attention(h, cosf, sinf, batch, seq):
    t = h.shape[0]
    bl = ATT_TOKENS
    nj = seq // bl
    hd = ATT_HEAD_DIM
    q_col = 3 * CONV_CH // hd
    k_col = q_col + ATT_HEADS
    v_col = k_col + ATT_HEADS

    def cur(col):
        return lambda b, hh, j: (b * nj + j, col + hh)

    def prev(col):
        return lambda b, hh, j: (b * nj + jnp.maximum(j - 1, 0), col + hh)

    rows_cur = lambda b, hh, j: (b * nj + j, 0)
    rows_prev = lambda b, hh, j: (b * nj + jnp.maximum(j - 1, 0), 0)
    blk = (bl, hd)
    return pl.pallas_call(
        _attn_kernel,
        out_shape=jax.ShapeDtypeStruct((t, ATT_WIDTH), BF16),
        grid=(batch, ATT_HEADS, nj),
        in_specs=[pl.BlockSpec(blk, cur(q_col)),
                  pl.BlockSpec(blk, cur(k_col)), pl.BlockSpec(blk, prev(k_col)),
                  pl.BlockSpec(blk, cur(v_col)), pl.BlockSpec(blk, prev(v_col)),
                  pl.BlockSpec(blk, rows_cur), pl.BlockSpec(blk, rows_cur),
                  pl.BlockSpec(blk, rows_prev), pl.BlockSpec(blk, rows_prev)],
        out_specs=pl.BlockSpec(blk, lambda b, hh, j: (b * nj + j, hh)),
        scratch_shapes=[pltpu.VMEM((bl, hd), F32), pltpu.VMEM((2 * bl, hd), F32),
                        pltpu.VMEM((2 * bl, hd), F32), pltpu.VMEM((bl, hd), F32),
                        pltpu.VMEM((bl, LANES), F32), pltpu.VMEM((bl, LANES), F32)],
        compiler_params=_params("parallel", "parallel", "parallel"),
        name="dilated_attention",
    )(h, h, h, h, h, cosf, sinf, cosf, sinf)


def _log_sigmoid(x):
    return jnp.minimum(x, 0.0) - jnp.log(1.0 + jnp.exp(-jnp.abs(x)))


def _mlstm_kernel(q_ref, k_ref, v_ref, og_ref, g_ref, bias_ref, o_ref, ct_s, n_s, m_s):
    nh, dk, dv, cl = MLSTM_HEADS, MLSTM_QK_DIM, MLSTM_V_DIM, MLSTM_CHUNK
    n_seq = q_ref.shape[0]

    @pl.when(pl.program_id(0) == 0)
    def _():
        ct_s[...] = jnp.zeros(ct_s.shape, F32)
        n_s[...] = jnp.zeros(n_s.shape, F32)
        m_s[...] = jnp.zeros(m_s.shape, F32)

    row = lax.broadcasted_iota(jnp.int32, (cl, cl), 0)
    col = lax.broadcasted_iota(jnp.int32, (cl, cl), 1)
    causal = col <= row
    ones = jnp.ones((cl, LANES), BF16)

    for b in range(n_seq):
        gates = g_ref[b] + bias_ref[...]
        gates_t = gates.T
        for hd in range(nh):
            st = b * nh + hd
            ig_col = gates[:, hd:hd + 1]
            ig_row = gates_t[hd:hd + 1, :]
            lf_col = _log_sigmoid(gates[:, nh + hd:nh + hd + 1])
            lf_row = _log_sigmoid(gates_t[nh + hd:nh + hd + 1, :])
            bcum_col = jnp.sum(jnp.where(causal, lf_row, 0.0), axis=1, keepdims=True)
            bcum_row = jnp.sum(jnp.where(row <= col, lf_col, 0.0), axis=0, keepdims=True)
            btot = bcum_col[cl - 1:cl, :]

            qb = q_ref[b, :, hd * dk:(hd + 1) * dk].astype(BF16)
            kf = k_ref[b, :, hd * dk:(hd + 1) * dk] * (dk ** -0.5)
            vf = v_ref[b, :, hd * dv:(hd + 1) * dv]
            kb = kf.astype(BF16)
            ct_prev = ct_s[st]
            n_prev = n_s[st]
            m_prev = m_s[st][:, 0:1]

            dmat = jnp.where(causal, bcum_col - bcum_row + ig_row, NEG_INF)
            g_inter = bcum_col + m_prev
            m_t = jnp.maximum(g_inter, jnp.max(dmat, axis=1, keepdims=True))
            qk = lax.dot_general(qb, kb, _TRANS_B, preferred_element_type=F32)
            sc = (qk * jnp.exp(dmat - m_t)).astype(BF16)
            a_inter = jnp.exp(g_inter - m_t)
            num = (a_inter * jnp.dot(qb, ct_prev.astype(BF16), preferred_element_type=F32)
                   + jnp.dot(sc, vf.astype(BF16), preferred_element_type=F32))
            n_rows = jnp.broadcast_to(n_prev, (LANES, dk)).astype(BF16)
            den = (a_inter * lax.dot_general(qb, n_rows, _TRANS_B, preferred_element_type=F32)
                   + jnp.dot(sc, ones, preferred_element_type=F32))
            scale = 1.0 / jnp.maximum(jnp.abs(den), jnp.exp(-m_t))
            hh = num * jnp.tile(scale, (1, dv // LANES))
            o_ref[b, :, hd * dv:(hd + 1) * dv] = (
                jax.nn.sigmoid(og_ref[b, :, hd * dv:(hd + 1) * dv]) * hh).astype(o_ref.dtype)

            w_loc = btot - bcum_col + ig_col
            m_loc = jnp.max(w_loc, axis=0, keepdims=True)
            a_loc = jnp.exp(w_loc - m_loc)
            ct_loc = jnp.dot(kf.T.astype(BF16), (vf * a_loc).astype(BF16),
                             preferred_element_type=F32)
            n_loc = jnp.sum(a_loc * kf, axis=0, keepdims=True)
            m_new = jnp.maximum(btot + m_prev, m_loc)
            a_old = jnp.exp(btot + m_prev - m_new)
            a_new = jnp.exp(m_loc - m_new)
            ct_s[st] = a_old * ct_prev + a_new * ct_loc
            n_s[st] = a_old * n_prev + a_new * n_loc
            m_s[st] = jnp.broadcast_to(m_new, (1, LANES))


def _mlstm(h, gates, gate_bias_row, batch, seq):
    t = h.shape[0]
    nh, dk, dv, cl = MLSTM_HEADS, MLSTM_QK_DIM, MLSTM_V_DIM, MLSTM_CHUNK
    h3 = h.reshape(batch, seq, h.shape[1])
    g3 = gates.reshape(batch, seq, LANES)
    cols = lambda col: (lambda c: (0, c, col))
    out = pl.pallas_call(
        _mlstm_kernel,
        out_shape=jax.ShapeDtypeStruct((batch, seq, nh * dv), BF16),
        grid=(seq // cl,),
        in_specs=[pl.BlockSpec((batch, cl, nh * dk), cols(0)),
                  pl.BlockSpec((batch, cl, nh * dk), cols(1)),
                  pl.BlockSpec((batch, cl, nh * dv), cols(1)),
                  pl.BlockSpec((batch, cl, nh * dv), cols(2)),
                  pl.BlockSpec((batch, cl, LANES), cols(0)),
                  pl.BlockSpec((1, LANES), lambda c: (0, 0))],
        out_specs=pl.BlockSpec((batch, cl, nh * dv), cols(0)),
        scratch_shapes=[pltpu.VMEM((batch * nh, dk, dv), F32),
                        pltpu.VMEM((batch * nh, 1, dk), F32),
                        pltpu.VMEM((batch * nh, 1, LANES), F32)],
        compiler_params=_params("arbitrary"),
        name="mlstm",
    )(h3, h3, h3, h3, g3, gate_bias_row)
    return out.reshape(t, nh * dv)


_E1_GROUPS = PEER_NKEYS // SUBLANES
_E1_ROWS_PER_PASS = 2
_BF16_PER_WORD = 2


def _pack_bf16(x):
    return pltpu.bitcast(x.astype(BF16), jnp.uint32)


def _unpack_bf16(words):
    return pltpu.bitcast(words, BF16)


def _bf16_rows(row, n_rows):
    tile_rows = SUBLANES * _BF16_PER_WORD
    return jnp.tile(jnp.broadcast_to(row, (tile_rows, row.shape[1])).astype(BF16),
                    (n_rows // tile_rows, 1))


def _top_values(s, k, with_rank=False):
    vals = []
    rank = jnp.full(s.shape, float(k), F32) if with_rank else None
    for it in range(k):
        mx = jnp.max(s, axis=0, keepdims=True)
        vals.append(mx)
        hit = s == mx
        if with_rank:
            rank = jnp.where(hit, float(it), rank)
        s = jnp.where(hit, NEG_INF, s)
    return vals, rank


def _peer_topk_kernel(q_ref, keys_ref, r2_ref, p2_ref, n_ref, p1_ref, s1_s, s2_s):
    kk = PEER_TOPK
    half = PEER_QDIM // 2
    tb = q_ref.shape[0]
    q = q_ref[...]
    s1_s[...] = lax.dot_general(keys_ref[0, 0], q[:, :half], _TRANS_B, preferred_element_type=F32)
    s2_s[...] = lax.dot_general(keys_ref[0, 1], q[:, half:], _TRANS_B, preferred_element_type=F32)

    def chunk(c):
        lanes = pl.ds(pl.multiple_of(c * LANES, LANES), LANES)
        s1 = s1_s[:, lanes]
        s2 = s2_s[:, lanes]
        v1, _ = _top_values(s1, kk)
        v2, rank2 = _top_values(s2, kk, with_rank=True)
        v2_all = jnp.concatenate(v2, axis=0)
        slabs = [v1[0] + v2_all]
        slabs += [v1[a] + v2_all[0:kk // 2] for a in range(1, kk // 2)]
        slabs += [jnp.concatenate(v1[kk // 2:], axis=0) + v2[0]]
        cand = jnp.concatenate(slabs, axis=0)
        m = v1[0] + v2[0]
        den = jnp.zeros_like(m)
        thr = m
        for _ in range(kk):
            thr = jnp.max(cand, axis=0, keepdims=True)
            den = den + jnp.exp(thr - m)
            cand = jnp.where(cand == thr, NEG_INF, cand)
        r2_ref[0, :, lanes] = _pack_bf16(rank2)
        p2_ref[0, :, lanes] = _pack_bf16(jnp.exp(s2 - v2[0]) / den)
        n = jnp.zeros(s1.shape, F32)
        for b in range(kk):
            n = n + jnp.where(s1 + v2[b] >= thr, 1.0, 0.0)
        p1 = jnp.exp(s1 - v1[0])
        for g in range(_E1_GROUPS):
            rows = slice(g * SUBLANES, (g + 1) * SUBLANES)
            n_ref[g, 0, :, lanes] = n[rows]
            p1_ref[g, 0, :, lanes] = p1[rows]

    def chunk_pair(i, carry):
        chunk(2 * i)
        chunk(2 * i + 1)
        return carry

    lax.fori_loop(0, tb // (2 * LANES), chunk_pair, 0)


def _peer_topk(qb, keys_b, tb=512):
    t = qb.shape[0]
    nh, nk = PEER_HEADS, PEER_NKEYS
    e2_spec = pl.BlockSpec((1, nk // _BF16_PER_WORD, tb), lambda i, h: (h, 0, i))
    e1_spec = pl.BlockSpec((_E1_GROUPS, 1, SUBLANES, tb), lambda i, h: (0, h, 0, i))
    return pl.pallas_call(
        _peer_topk_kernel,
        out_shape=(jax.ShapeDtypeStruct((nh, nk // _BF16_PER_WORD, t), jnp.uint32),
                   jax.ShapeDtypeStruct((nh, nk // _BF16_PER_WORD, t), jnp.uint32),
                   jax.ShapeDtypeStruct((_E1_GROUPS, nh, SUBLANES, t), F32),
                   jax.ShapeDtypeStruct((_E1_GROUPS, nh, SUBLANES, t), F32)),
        grid=(t // tb, nh),
        in_specs=[pl.BlockSpec((tb, PEER_QDIM), lambda i, h: (i, h)),
                  pl.BlockSpec((1, 2, nk, PEER_QDIM // 2), lambda i, h: (h, 0, 0, 0))],
        out_specs=(e2_spec, e2_spec, e1_spec, e1_spec),
        scratch_shapes=[pltpu.VMEM((nk, tb), F32), pltpu.VMEM((nk, tb), F32)],
        compiler_params=_params("parallel", "parallel"),
        name="peer_topk",
    )(qb, keys_b)


def _gelu_exact(x):
    return 0.5 * x * (1.0 + lax.erf(x * (2.0 ** -0.5)))


def _zero_after(v):
    bits = pltpu.bitcast(v, jnp.uint32)
    return pltpu.bitcast((bits >> 16) >> 16, F32)[0:1, :]


def _peer_tile_activations(hid_s, act_s, group, row0, r2_ref, p2_ref, n_ref, p1_ref, after):
    nh, nk = PEER_HEADS, PEER_NKEYS
    te, tb = hid_s.shape
    zero = jnp.zeros((nk, LANES), BF16)
    for c in range(tb // LANES):
        lanes = slice(c * LANES, (c + 1) * LANES)
        for le1_0 in range(0, te // nk, _E1_ROWS_PER_PASS):
            le1s = range(le1_0, le1_0 + _E1_ROWS_PER_PASS)
            w = [zero for _ in le1s]
            for h in range(nh):
                rank2 = _unpack_bf16(r2_ref[h, :, lanes])
                p2 = _unpack_bf16(p2_ref[h, :, lanes])
                for j, le1 in enumerate(le1s):
                    r = row0 + le1
                    n_rows = _bf16_rows(n_ref[group, h, r:r + 1, lanes] + after, nk)
                    p1_rows = _bf16_rows(p1_ref[group, h, r:r + 1, lanes], nk)
                    w[j] = w[j] + jnp.where(rank2 < n_rows, p2 * p1_rows, zero)
            for j, le1 in enumerate(le1s):
                rows = slice(le1 * nk, (le1 + 1) * nk)
                act = w[j] * _gelu_exact(hid_s[rows, lanes]).astype(BF16)
                act_s[rows, lanes] = act
            after = _zero_after(act[0:2 * SUBLANES, :])
    return after


def _peer_dense_kernel(x_ref, u_ref, vt_ref, r2_ref, p2_ref, n_ref, p1_ref, o_ref,
                       acc_s, hid0_s, hid1_s, act0_s, act1_s):
    te = hid0_s.shape[0]
    rows_per_tile = te // PEER_NKEYS
    s = pl.program_id(1)
    last = pl.num_programs(1) - 1
    no_dep = jnp.zeros((1, LANES), F32)

    def hidden(hid_s, half, x):
        hid_s[...] = lax.dot_general(u_ref[half * te:(half + 1) * te, :], x, _TRANS_B,
                                     preferred_element_type=F32)

    def accumulate(act_s, half):
        acc_s[...] += jnp.dot(vt_ref[:, half * te:(half + 1) * te], act_s[...],
                              preferred_element_type=F32)

    def activations(hid_s, act_s, group, row0, after):
        return _peer_tile_activations(hid_s, act_s, group, row0, r2_ref, p2_ref, n_ref, p1_ref,
                                      after)

    @pl.when(s == 0)
    def _():
        acc_s[...] = jnp.zeros(acc_s.shape, F32)
        x = x_ref[...]
        hidden(hid0_s, 0, x)
        hidden(hid1_s, 1, x)
        activations(hid0_s, act0_s, 0, 0, no_dep)

    @pl.when((s > 0) & (s < last))
    def _():
        x = x_ref[...]
        after = activations(hid1_s, act1_s, s - 1, rows_per_tile, no_dep)
        accumulate(act0_s, 0)
        hidden(hid0_s, 0, x)
        hidden(hid1_s, 1, x)
        activations(hid0_s, act0_s, s, 0, after)
        accumulate(act1_s, 1)

    @pl.when(s == last)
    def _():
        activations(hid1_s, act1_s, s - 1, rows_per_tile, no_dep)
        accumulate(act0_s, 0)
        accumulate(act1_s, 1)
        o_ref[...] = acc_s[...].T


def _peer_dense(xb, u_b, v_b, r2, p2, n_rows, p1_rows, tb=512):
    t, dm = xb.shape
    nh, nk = PEER_HEADS, PEER_NKEYS
    ne = u_b.shape[0]
    te = SUBLANES * nk // 2
    n_pairs = ne // (2 * te)
    assert n_pairs == _E1_GROUPS
    vt_b = v_b.reshape(n_pairs, 2 * te, dm).transpose(0, 2, 1)
    e2_spec = pl.BlockSpec((nh, nk // _BF16_PER_WORD, tb), lambda i, s: (0, 0, i))
    e1_spec = pl.BlockSpec((_E1_GROUPS, nh, SUBLANES, tb), lambda i, s: (0, 0, 0, i))
    return pl.pallas_call(
        _peer_dense_kernel,
        out_shape=jax.ShapeDtypeStruct((t, dm), F32),
        grid=(t // tb, n_pairs + 1),
        in_specs=[pl.BlockSpec((tb, dm), lambda i, s: (i, 0)),
                  pl.BlockSpec((2 * te, dm), lambda i, s: (jnp.minimum(s, n_pairs - 1), 0)),
                  pl.BlockSpec((None, dm, 2 * te), lambda i, s: (jnp.maximum(s - 1, 0), 0, 0)),
                  e2_spec, e2_spec, e1_spec, e1_spec],
        out_specs=pl.BlockSpec((tb, dm), lambda i, s: (i, 0)),
        scratch_shapes=[pltpu.VMEM((dm, tb), F32),
                        pltpu.VMEM((te, tb), F32), pltpu.VMEM((te, tb), F32),
                        pltpu.VMEM((te, tb), BF16), pltpu.VMEM((te, tb), BF16)],
        compiler_params=_params("parallel", "arbitrary"),
        name="peer_dense",
    )(xb, u_b, vt_b, r2, p2, n_rows, p1_rows)


def _peer(xb, w_q_b, keys_b, u_b, v_b):
    qb = _matmul(xb, w_q_b, BF16)
    r2, p2, n_rows, p1_rows = _peer_topk(qb, keys_b)
    return _peer_dense(xb, u_b, v_b, r2, p2, n_rows, p1_rows)


def _rope_tables(positions):
    half = ATT_HEAD_DIM // 2
    inv_freq = jnp.power(ROPE_THETA, -2.0 * jnp.arange(half, dtype=F32) / ATT_HEAD_DIM)
    ang = positions.astype(F32).reshape(-1, 1) * inv_freq
    cos, sin = jnp.cos(ang), jnp.sin(ang)
    return jnp.concatenate([cos, cos], axis=-1), jnp.concatenate([-sin, sin], axis=-1)


def kernel(x, p, positions, even_w_in, even_conv_w, even_w_out, odd_w_in, odd_gate_bias,
           odd_w_out, peer_w_q, peer_sub_keys, peer_u, peer_v, ple_w_proj, ple_w_gate,
           ln_gain, ln_bias):
    batch, seq, dm = x.shape
    t = batch * seq
    assert dm == D_MODEL and seq % ATT_TOKENS == 0 and seq % MLSTM_CHUNK == 0
    xf = x.reshape(t, dm)
    xb = xf.astype(BF16)
    cosf, sinf = _rope_tables(positions)

    for layer in range(DEPTH):
        j = layer // 2
        if layer % 2 == 0:
            h = _matmul(xb, even_w_in[j].astype(BF16), F32)
            y_conv = _conv_gate(h, even_conv_w[j], seq)
            y_att = _dilated_attention(h, cosf, sinf, batch, seq)
            mixed = [y_conv, y_att]
            w_out = even_w_out[j]
        else:
            w_in = odd_w_in[j]
            h = _matmul(xb, w_in[:, :ODD_MAIN].astype(BF16), F32)
            n_gates = 2 * MLSTM_HEADS
            w_gates = jnp.pad(w_in[:, ODD_MAIN:], ((0, 0), (0, LANES - n_gates))).astype(BF16)
            gates = _matmul(xb, w_gates, F32)
            bias_row = jnp.pad(odd_gate_bias[j].reshape(1, n_gates), ((0, 0), (0, LANES - n_gates)))
            mixed = [_mlstm(h, gates, bias_row, batch, seq)]
            w_out = odd_w_out[j]
        xf, xb = _matmul_residual_ln(mixed, w_out.astype(BF16), xf,
                                     ln_gain[layer, 0], ln_bias[layer, 0])
        ffn = _peer(xb, peer_w_q[layer].astype(BF16), peer_sub_keys[layer].astype(BF16),
                    peer_u[layer].astype(BF16), peer_v[layer].astype(BF16))
        xf, xb = _ple_residual_ln(xf, xb, ffn, p[layer].reshape(t, PLE_DIM).astype(BF16),
                                  ple_w_gate[layer].astype(BF16), ple_w_proj[layer].astype(BF16),
                                  ln_gain[layer, 1], ln_bias[layer, 1])
    return xf.reshape(batch, seq, dm)
```

```python
import functools
import math

import jax
import jax.numpy as jnp
from jax import lax
from jax.experimental import pallas as pl
from jax.experimental.pallas import tpu as pltpu

D_MODEL = 2048
DEPTH = 4
CONV_CH = D_MODEL // 2
CONV_WIDTH = 3
ATT_HEADS = 8
ATT_HEAD_DIM = 128
ATT_WIDTH = ATT_HEADS * ATT_HEAD_DIM
DILATED_PAIRS = ((128, 1), (512, 4), (2048, 16))
ROPE_THETA = 10000.0
EVEN_IN = 3 * CONV_CH + 3 * ATT_WIDTH
MLSTM_HEADS = 8
MLSTM_QK_DIM = 128
MLSTM_V_DIM = D_MODEL // MLSTM_HEADS
MLSTM_CHUNK = 128
ODD_MAIN = 2 * MLSTM_HEADS * MLSTM_QK_DIM + 2 * MLSTM_HEADS * MLSTM_V_DIM
PEER_HEADS = 8
PEER_NKEYS = 128
PEER_QDIM = 256
PEER_TOPK = 16
PEER_EXPERTS = PEER_NKEYS * PEER_NKEYS
PLE_DIM = 256
ALPHA = (2.0 * DEPTH) ** 0.25
LN_EPS = 1e-5

LANES = 128
SUBLANES = 8
VMEM_LIMIT_BYTES = 56 * 1024 * 1024

ATT_BLOCK = 128
ATT_SPAN = 128
ATT_TOKENS = ATT_BLOCK * max(d for _, d in DILATED_PAIRS)

NEG_INF = float("-inf")
F32 = jnp.float32
BF16 = jnp.bfloat16

_TRANS_B = (((1,), (1,)), ((), ()))


def _params(*semantics):
    return pltpu.CompilerParams(dimension_semantics=semantics,
                                vmem_limit_bytes=VMEM_LIMIT_BYTES)


def _layer_norm_rows(y, gain, bias):
    mu = jnp.mean(y, axis=-1, keepdims=True)
    yc = y - mu
    var = jnp.mean(yc * yc, axis=-1, keepdims=True)
    return yc * lax.rsqrt(var + LN_EPS) * gain + bias


def _mm_kernel(a_ref, w_ref, o_ref):
    o_ref[...] = jnp.dot(a_ref[...], w_ref[...],
                         preferred_element_type=F32).astype(o_ref.dtype)


def _matmul(a, w, out_dtype, tm=512, tn=None):
    m, k = a.shape
    n = w.shape[1]
    tn = min(n, 2048) if tn is None else tn
    return pl.pallas_call(
        _mm_kernel,
        out_shape=jax.ShapeDtypeStruct((m, n), out_dtype),
        grid=(n // tn, m // tm),
        in_specs=[pl.BlockSpec((tm, k), lambda j, i: (i, 0)),
                  pl.BlockSpec((k, tn), lambda j, i: (0, j))],
        out_specs=pl.BlockSpec((tm, tn), lambda j, i: (i, j)),
        compiler_params=_params("parallel", "parallel"),
        name="matmul",
    )(a, w)


def _mm_ln_kernel(*refs, n_a):
    a_refs = refs[:n_a]
    w_ref, res_ref, g_ref, b_ref, o_ref, ob_ref = refs[n_a:]
    acc = None
    off = 0
    for a_ref in a_refs:
        k = a_ref.shape[1]
        part = jnp.dot(a_ref[...], w_ref[off:off + k, :], preferred_element_type=F32)
        acc = part if acc is None else acc + part
        off += k
    y = ALPHA * res_ref[...] + acc
    out = _layer_norm_rows(y, g_ref[...], b_ref[...])
    o_ref[...] = out
    ob_ref[...] = out.astype(BF16)


def _matmul_residual_ln(a_list, w, res, gain, bias, tm=256):
    m = res.shape[0]
    n = w.shape[1]
    in_specs = [pl.BlockSpec((tm, a.shape[1]), lambda i: (i, 0)) for a in a_list]
    in_specs += [pl.BlockSpec(w.shape, lambda i: (0, 0)),
                 pl.BlockSpec((tm, n), lambda i: (i, 0)),
                 pl.BlockSpec((1, n), lambda i: (0, 0)),
                 pl.BlockSpec((1, n), lambda i: (0, 0))]
    return pl.pallas_call(
        functools.partial(_mm_ln_kernel, n_a=len(a_list)),
        out_shape=(jax.ShapeDtypeStruct((m, n), F32), jax.ShapeDtypeStruct((m, n), BF16)),
        grid=(m // tm,),
        in_specs=in_specs,
        out_specs=(pl.BlockSpec((tm, n), lambda i: (i, 0)),
                   pl.BlockSpec((tm, n), lambda i: (i, 0))),
        compiler_params=_params("parallel"),
        name="matmul_residual_ln",
    )(*a_list, w, res, gain.reshape(1, n), bias.reshape(1, n))


def _ple_ln_kernel(x_ref, xb_ref, ffn_ref, p_ref, wg_ref, wp_ref, g_ref, b_ref, o_ref, ob_ref):
    gate = jax.nn.sigmoid(jnp.dot(xb_ref[...], wg_ref[...], preferred_element_type=F32))
    proj = jnp.dot(p_ref[...], wp_ref[...], preferred_element_type=F32)
    y = ALPHA * x_ref[...] + ffn_ref[...] + gate * proj
    out = _layer_norm_rows(y, g_ref[...], b_ref[...])
    o_ref[...] = out
    ob_ref[...] = out.astype(BF16)


def _ple_residual_ln(x, xb, ffn, p, wg, wp, gain, bias, tm=256):
    m, n = x.shape
    row = lambda i: (i, 0)
    const = lambda i: (0, 0)
    return pl.pallas_call(
        _ple_ln_kernel,
        out_shape=(jax.ShapeDtypeStruct((m, n), F32), jax.ShapeDtypeStruct((m, n), BF16)),
        grid=(m // tm,),
        in_specs=[pl.BlockSpec((tm, n), row), pl.BlockSpec((tm, n), row),
                  pl.BlockSpec((tm, n), row), pl.BlockSpec((tm, p.shape[1]), row),
                  pl.BlockSpec(wg.shape, const), pl.BlockSpec(wp.shape, const),
                  pl.BlockSpec((1, n), const), pl.BlockSpec((1, n), const)],
        out_specs=(pl.BlockSpec((tm, n), row), pl.BlockSpec((tm, n), row)),
        compiler_params=_params("parallel"),
        name="ple_residual_ln",
    )(x, xb, ffn, p, wg, wp, gain.reshape(1, n), bias.reshape(1, n))


def _conv_gate_kernel(gb_ref, gc_ref, xa_ref, w_ref, o_ref, buf, *, steps_per_seq):
    tm = gb_ref.shape[0]
    halo = SUBLANES

    @pl.when(pl.program_id(0) % steps_per_seq == 0)
    def _():
        buf[0:halo, :] = jnp.zeros((halo, buf.shape[1]), F32)

    u = gc_ref[...] * xa_ref[...]
    buf[halo:halo + tm, :] = u
    w = w_ref[...]
    conv = (w[0:1, :] * buf[halo - 2:halo - 2 + tm, :]
            + w[1:2, :] * buf[halo - 1:halo - 1 + tm, :]
            + w[2:3, :] * u)
    o_ref[...] = (gb_ref[...] * conv).astype(o_ref.dtype)
    buf[0:halo, :] = buf[tm:tm + halo, :]


def _conv_gate(h, conv_w, seq, tm=512):
    t = h.shape[0]
    ch = CONV_CH
    return pl.pallas_call(
        functools.partial(_conv_gate_kernel, steps_per_seq=seq // tm),
        out_shape=jax.ShapeDtypeStruct((t, ch), BF16),
        grid=(t // tm,),
        in_specs=[pl.BlockSpec((tm, ch), lambda i: (i, 0)),
                  pl.BlockSpec((tm, ch), lambda i: (i, 1)),
                  pl.BlockSpec((tm, ch), lambda i: (i, 2)),
                  pl.BlockSpec((CONV_WIDTH, ch), lambda i: (0, 0))],
        out_specs=pl.BlockSpec((tm, ch), lambda i: (i, 0)),
        scratch_shapes=[pltpu.VMEM((tm + SUBLANES, ch), F32)],
        compiler_params=_params("arbitrary"),
        name="conv_gate",
    )(h, h, h, conv_w)


def _rope(t, cosf, sinf):
    return t * cosf + pltpu.roll(t, ATT_HEAD_DIM // 2, 1) * sinf


def _attn_kernel(q_ref, kc_ref, kp_ref, vc_ref, vp_ref, cc_ref, sc_ref, cp_ref, sp_ref,
                 o_ref, q_s, k_s, v_s, acc_s, m_s, l_s):
    j = pl.program_id(2)
    bl = ATT_TOKENS
    blk = ATT_BLOCK
    cc = cc_ref[...]
    sc = sc_ref[...]
    q_s[...] = _rope(q_ref[...], cc, sc)
    k_s[bl:2 * bl, :] = _rope(kc_ref[...], cc, sc)
    k_s[0:bl, :] = _rope(kp_ref[...], cp_ref[...], sp_ref[...])
    v_s[bl:2 * bl, :] = vc_ref[...]
    v_s[0:bl, :] = vp_ref[...]

    qi = lax.broadcasted_iota(jnp.int32, (blk, 2 * blk), 0)
    kj = lax.broadcasted_iota(jnp.int32, (blk, 2 * blk), 1)
    dist = qi + blk - kj
    band = (dist >= 0) & (dist <= ATT_SPAN)
    band_first = band & (kj >= jnp.where(j > 0, 0, blk))
    scale = ATT_HEAD_DIM ** -0.5

    for branch, (_, d) in enumerate(DILATED_PAIRS):
        for g in range(bl // (blk * d)):
            for r in range(d):
                q_rows = pl.ds(g * blk * d + r, blk, stride=d) if d > 1 else pl.ds(g * blk, blk)
                k_start = bl + (g - 1) * blk * d + r
                k_rows = pl.ds(k_start, 2 * blk, stride=d) if d > 1 else pl.ds(k_start, 2 * blk)
                qs = q_s[q_rows, :].astype(BF16)
                ks = k_s[k_rows, :].astype(BF16)
                vs = v_s[k_rows, :].astype(BF16)
                s = lax.dot_general(qs, ks, _TRANS_B, preferred_element_type=F32) * scale
                s = jnp.where(band_first if g == 0 else band, s, NEG_INF)
                m_b = jnp.max(s, axis=-1, keepdims=True)
                e = jnp.exp(s - m_b)
                l_b = jnp.sum(e, axis=-1, keepdims=True)
                o_b = jnp.dot(e.astype(BF16), vs, preferred_element_type=F32)
                if branch == 0:
                    acc_s[q_rows, :] = o_b
                    m_s[q_rows, :] = jnp.broadcast_to(m_b, (blk, LANES))
                    l_s[q_rows, :] = jnp.broadcast_to(l_b, (blk, LANES))
                else:
                    m_old = m_s[q_rows, :]
                    m_new = jnp.maximum(m_old, m_b)
                    a_old = jnp.exp(m_old - m_new)
                    a_new = jnp.exp(m_b - m_new)
                    acc_s[q_rows, :] = a_old * acc_s[q_rows, :] + a_new * o_b
                    l_s[q_rows, :] = a_old * l_s[q_rows, :] + a_new * l_b
                    m_s[q_rows, :] = m_new

    o_ref[...] = (acc_s[...] / l_s[...]).astype(o_ref.dtype)


def _dilated_attention(h, cosf, sinf, batch, seq):
    t = h.shape[0]
    bl = ATT_TOKENS
    nj = seq // bl
    hd = ATT_HEAD_DIM
    q_col = 3 * CONV_CH // hd
    k_col = q_col + ATT_HEADS
    v_col = k_col + ATT_HEADS

    def cur(col):
        return lambda b, hh, j: (b * nj + j, col + hh)

    def prev(col):
        return lambda b, hh, j: (b * nj + jnp.maximum(j - 1, 0), col + hh)

    rows_cur = lambda b, hh, j: (b * nj + j, 0)
    rows_prev = lambda b, hh, j: (b * nj + jnp.maximum(j - 1, 0), 0)
    blk = (bl, hd)
    return pl.pallas_call(
        _attn_kernel,
        out_shape=jax.ShapeDtypeStruct((t, ATT_WIDTH), BF16),
        grid=(batch, ATT_HEADS, nj),
        in_specs=[pl.BlockSpec(blk, cur(q_col)),
                  pl.BlockSpec(blk, cur(k_col)), pl.BlockSpec(blk, prev(k_col)),
                  pl.BlockSpec(blk, cur(v_col)), pl.BlockSpec(blk, prev(v_col)),
                  pl.BlockSpec(blk, rows_cur), pl.BlockSpec(blk, rows_cur),
                  pl.BlockSpec(blk, rows_prev), pl.BlockSpec(blk, rows_prev)],
        out_specs=pl.BlockSpec(blk, lambda b, hh, j: (b * nj + j, hh)),
        scratch_shapes=[pltpu.VMEM((bl, hd), F32), pltpu.VMEM((2 * bl, hd), F32),
                        pltpu.VMEM((2 * bl, hd), F32), pltpu.VMEM((bl, hd), F32),
                        pltpu.VMEM((bl, LANES), F32), pltpu.VMEM((bl, LANES), F32)],
        compiler_params=_params("parallel", "parallel", "parallel"),
        name="dilated_attention",
    )(h, h, h, h, h, cosf, sinf, cosf, sinf)


def _log_sigmoid(x):
    return jnp.minimum(x, 0.0) - jnp.log(1.0 + jnp.exp(-jnp.abs(x)))


def _mlstm_kernel(q_ref, k_ref, v_ref, og_ref, g_ref, bias_ref, o_ref, ct_s, n_s, m_s):
    nh, dk, dv, cl = MLSTM_HEADS, MLSTM_QK_DIM, MLSTM_V_DIM, MLSTM_CHUNK
    n_seq = q_ref.shape[0]

    @pl.when(pl.program_id(0) == 0)
    def _():
        ct_s[...] = jnp.zeros(ct_s.shape, F32)
        n_s[...] = jnp.zeros(n_s.shape, F32)
        m_s[...] = jnp.zeros(m_s.shape, F32)

    row = lax.broadcasted_iota(jnp.int32, (cl, cl), 0)
    col = lax.broadcasted_iota(jnp.int32, (cl, cl), 1)
    causal = col <= row
    ones = jnp.ones((cl, LANES), BF16)

    for b in range(n_seq):
        gates = g_ref[b] + bias_ref[...]
        gates_t = gates.T
        for hd in range(nh):
            st = b * nh + hd
            ig_col = gates[:, hd:hd + 1]
            ig_row = gates_t[hd:hd + 1, :]
            lf_col = _log_sigmoid(gates[:, nh + hd:nh + hd + 1])
            lf_row = _log_sigmoid(gates_t[nh + hd:nh + hd + 1, :])
            bcum_col = jnp.sum(jnp.where(causal, lf_row, 0.0), axis=1, keepdims=True)
            bcum_row = jnp.sum(jnp.where(row <= col, lf_col, 0.0), axis=0, keepdims=True)
            btot = bcum_col[cl - 1:cl, :]

            qb = q_ref[b, :, hd * dk:(hd + 1) * dk].astype(BF16)
            kf = k_ref[b, :, hd * dk:(hd + 1) * dk] * (dk ** -0.5)
            vf = v_ref[b, :, hd * dv:(hd + 1) * dv]
            kb = kf.astype(BF16)
            ct_prev = ct_s[st]
            n_prev = n_s[st]
            m_prev = m_s[st][:, 0:1]

            dmat = jnp.where(causal, bcum_col - bcum_row + ig_row, NEG_INF)
            g_inter = bcum_col + m_prev
            m_t = jnp.maximum(g_inter, jnp.max(dmat, axis=1, keepdims=True))
            qk = lax.dot_general(qb, kb, _TRANS_B, preferred_element_type=F32)
            sc = (qk * jnp.exp(dmat - m_t)).astype(BF16)
            a_inter = jnp.exp(g_inter - m_t)
            num = (a_inter * jnp.dot(qb, ct_prev.astype(BF16), preferred_element_type=F32)
                   + jnp.dot(sc, vf.astype(BF16), preferred_element_type=F32))
            n_rows = jnp.broadcast_to(n_prev, (LANES, dk)).astype(BF16)
            den = (a_inter * lax.dot_general(qb, n_rows, _TRANS_B, preferred_element_type=F32)
                   + jnp.dot(sc, ones, preferred_element_type=F32))
            scale = 1.0 / jnp.maximum(jnp.abs(den), jnp.exp(-m_t))
            hh = num * jnp.tile(scale, (1, dv // LANES))
            o_ref[b, :, hd * dv:(hd + 1) * dv] = (
                jax.nn.sigmoid(og_ref[b, :, hd * dv:(hd + 1) * dv]) * hh).astype(o_ref.dtype)

            w_loc = btot - bcum_col + ig_col
            m_loc = jnp.max(w_loc, axis=0, keepdims=True)
            a_loc = jnp.exp(w_loc - m_loc)
            ct_loc = jnp.dot(kf.T.astype(BF16), (vf * a_loc).astype(BF16),
                             preferred_element_type=F32)
            n_loc = jnp.sum(a_loc * kf, axis=0, keepdims=True)
            m_new = jnp.maximum(btot + m_prev, m_loc)
            a_old = jnp.exp(btot + m_prev - m_new)
            a_new = jnp.exp(m_loc - m_new)
            ct_s[st] = a_old * ct_prev + a_new * ct_loc
            n_s[st] = a_old * n_prev + a_new * n_loc
            m_s[st] = jnp.broadcast_to(m_new, (1, LANES))


def _mlstm(h, gates, gate_bias_row, batch, seq):
    t = h.shape[0]
    nh, dk, dv, cl = MLSTM_HEADS, MLSTM_QK_DIM, MLSTM_V_DIM, MLSTM_CHUNK
    h3 = h.reshape(batch, seq, h.shape[1])
    g3 = gates.reshape(batch, seq, LANES)
    cols = lambda col: (lambda c: (0, c, col))
    out = pl.pallas_call(
        _mlstm_kernel,
        out_shape=jax.ShapeDtypeStruct((batch, seq, nh * dv), BF16),
        grid=(seq // cl,),
        in_specs=[pl.BlockSpec((batch, cl, nh * dk), cols(0)),
                  pl.BlockSpec((batch, cl, nh * dk), cols(1)),
                  pl.BlockSpec((batch, cl, nh * dv), cols(1)),
                  pl.BlockSpec((batch, cl, nh * dv), cols(2)),
                  pl.BlockSpec((batch, cl, LANES), cols(0)),
                  pl.BlockSpec((1, LANES), lambda c: (0, 0))],
        out_specs=pl.BlockSpec((batch, cl, nh * dv), cols(0)),
        scratch_shapes=[pltpu.VMEM((batch * nh, dk, dv), F32),
                        pltpu.VMEM((batch * nh, 1, dk), F32),
                        pltpu.VMEM((batch * nh, 1, LANES), F32)],
        compiler_params=_params("arbitrary"),
        name="mlstm",
    )(h3, h3, h3, h3, g3, gate_bias_row)
    return out.reshape(t, nh * dv)


_E1_GROUPS = PEER_NKEYS // SUBLANES
_E1_ROWS_PER_PASS = 2
_BF16_PER_WORD = 2


def _pack_bf16(x):
    return pltpu.bitcast(x.astype(BF16), jnp.uint32)


def _unpack_bf16(words):
    return pltpu.bitcast(words, BF16)


def _bf16_rows(row, n_rows):
    tile_rows = SUBLANES * _BF16_PER_WORD
    return jnp.tile(jnp.broadcast_to(row, (tile_rows, row.shape[1])).astype(BF16),
                    (n_rows // tile_rows, 1))


def _sorting_network(n):
    pairs = []
    p = 1
    while p < n:
        k = p
        while k >= 1:
            for j in range(k % p, n - k, 2 * k):
                for i in range(min(k, n - j - k)):
                    if (i + j) // (2 * p) == (i + j + k) // (2 * p):
                        pairs.append((i + j, i + j + k))
            k //= 2
        p *= 2
    return pairs


def _pop_sorted_columns(levels, k, extra=None, on_pop=None):
    levels = list(levels)
    depth = len(levels)
    vals = []
    for t in range(k):
        top = levels[0] if extra is None else jnp.maximum(levels[0], extra)
        mx = jnp.max(top, axis=0, keepdims=True)
        vals.append(mx)
        if on_pop is not None:
            on_pop(mx)
        hit = levels[0] == mx
        if extra is not None:
            extra = jnp.where(extra == mx, NEG_INF, extra)
        for lv in range(min(depth, k - t)):
            below = levels[lv + 1] if lv + 1 < depth else NEG_INF
            levels[lv] = jnp.where(hit, below, levels[lv])
    return vals


def _top_values(s, k):
    levels = [s[i * SUBLANES:(i + 1) * SUBLANES] for i in range(s.shape[0] // SUBLANES)]
    for i, j in _sorting_network(len(levels)):
        levels[i], levels[j] = jnp.maximum(levels[i], levels[j]), jnp.minimum(levels[i], levels[j])
    return _pop_sorted_columns(levels, k)


def _peer_topk_kernel(q_ref, keys_ref, r2_ref, p2_ref, n_ref, p1_ref, s1_s, s2_s):
    kk = PEER_TOPK
    assert kk == 2 * SUBLANES
    half = PEER_QDIM // 2
    tb = q_ref.shape[0]
    q = q_ref[...]
    s1_s[...] = lax.dot_general(keys_ref[0, 0], q[:, :half], _TRANS_B, preferred_element_type=F32)
    s2_s[...] = lax.dot_general(keys_ref[0, 1], q[:, half:], _TRANS_B, preferred_element_type=F32)

    def chunk(c):
        lanes = pl.ds(pl.multiple_of(c * LANES, LANES), LANES)
        s1 = s1_s[:, lanes]
        s2 = s2_s[:, lanes]
        v1 = _top_values(s1, kk)
        v2 = _top_values(s2, kk)
        v2_lo = jnp.concatenate(v2[:SUBLANES], axis=0)
        v2_hi = jnp.concatenate(v2[SUBLANES:], axis=0)
        m = v1[0] + v2[0]
        exps = []
        sums = _pop_sorted_columns([v1[a] + v2_lo for a in range(kk)], kk, extra=v1[0] + v2_hi,
                                   on_pop=lambda mx: exps.append(jnp.exp(mx - m)))
        thr = sums[-1]
        den = functools.reduce(lambda a, b: a + b, exps)
        p2_ref[0, :, lanes] = _pack_bf16(jnp.exp(s2 - v2[0]) / den)
        rank2 = jnp.zeros(s2.shape, F32)
        n = jnp.zeros(s1.shape, F32)
        for b in range(kk):
            rank2 = rank2 + jnp.where(s2 < v2[b], 1.0, 0.0)
            n = n + jnp.where(s1 + v2[b] >= thr, 1.0, 0.0)
        r2_ref[0, :, lanes] = _pack_bf16(rank2)
        p1 = jnp.exp(s1 - v1[0])
        for g in range(_E1_GROUPS):
            rows = slice(g * SUBLANES, (g + 1) * SUBLANES)
            n_ref[g, 0, :, lanes] = n[rows]
            p1_ref[g, 0, :, lanes] = p1[rows]

    def chunk_pair(i, carry):
        chunk(2 * i)
        chunk(2 * i + 1)
        return carry

    lax.fori_loop(0, tb // (2 * LANES), chunk_pair, 0)


def _peer_topk(qb, keys_b, tb=512):
    t = qb.shape[0]
    nh, nk = PEER_HEADS, PEER_NKEYS
    e2_spec = pl.BlockSpec((1, nk // _BF16_PER_WORD, tb), lambda i, h: (h, 0, i))
    e1_spec = pl.BlockSpec((_E1_GROUPS, 1, SUBLANES, tb), lambda i, h: (0, h, 0, i))
    return pl.pallas_call(
        _peer_topk_kernel,
        out_shape=(jax.ShapeDtypeStruct((nh, nk // _BF16_PER_WORD, t), jnp.uint32),
                   jax.ShapeDtypeStruct((nh, nk // _BF16_PER_WORD, t), jnp.uint32),
                   jax.ShapeDtypeStruct((_E1_GROUPS, nh, SUBLANES, t), F32),
                   jax.ShapeDtypeStruct((_E1_GROUPS, nh, SUBLANES, t), F32)),
        grid=(t // tb, nh),
        in_specs=[pl.BlockSpec((tb, PEER_QDIM), lambda i, h: (i, h)),
                  pl.BlockSpec((1, 2, nk, PEER_QDIM // 2), lambda i, h: (h, 0, 0, 0))],
        out_specs=(e2_spec, e2_spec, e1_spec, e1_spec),
        scratch_shapes=[pltpu.VMEM((nk, tb), F32), pltpu.VMEM((nk, tb), F32)],
        compiler_params=_params("parallel", "parallel"),
        name="peer_topk",
    )(qb, keys_b)


def _gelu_exact(x):
    return 0.5 * x * (1.0 + lax.erf(x * (2.0 ** -0.5)))


def _zero_after(v):
    bits = pltpu.bitcast(v, jnp.uint32)
    return pltpu.bitcast((bits >> 16) >> 16, F32)[0:1, :]


def _peer_tile_activations(hid_s, act_s, group, row0, r2_ref, p2_ref, n_ref, p1_ref, after):
    nh, nk = PEER_HEADS, PEER_NKEYS
    te, tb = hid_s.shape
    zero = jnp.zeros((nk, LANES), BF16)
    for c in range(tb // LANES):
        lanes = slice(c * LANES, (c + 1) * LANES)
        for le1_0 in range(0, te // nk, _E1_ROWS_PER_PASS):
            le1s = range(le1_0, le1_0 + _E1_ROWS_PER_PASS)
            w = [zero for _ in le1s]
            for h in range(nh):
                rank2 = _unpack_bf16(r2_ref[h, :, lanes])
                p2 = _unpack_bf16(p2_ref[h, :, lanes])
                for j, le1 in enumerate(le1s):
                    r = row0 + le1
                    n_rows = _bf16_rows(n_ref[group, h, r:r + 1, lanes] + after, nk)
                    p1_rows = _bf16_rows(p1_ref[group, h, r:r + 1, lanes], nk)
                    w[j] = w[j] + jnp.where(rank2 < n_rows, p2 * p1_rows, zero)
            for j, le1 in enumerate(le1s):
                rows = slice(le1 * nk, (le1 + 1) * nk)
                act = w[j] * _gelu_exact(hid_s[rows, lanes]).astype(BF16)
                act_s[rows, lanes] = act
            after = _zero_after(act[0:2 * SUBLANES, :])
    return after


def _peer_dense_kernel(x_ref, u_ref, vt_ref, r2_ref, p2_ref, n_ref, p1_ref, o_ref,
                       acc_s, hid0_s, hid1_s, act0_s, act1_s):
    te = hid0_s.shape[0]
    rows_per_tile = te // PEER_NKEYS
    s = pl.program_id(1)
    last = pl.num_programs(1) - 1
    no_dep = jnp.zeros((1, LANES), F32)

    def hidden(hid_s, half, x):
        hid_s[...] = lax.dot_general(u_ref[half * te:(half + 1) * te, :], x, _TRANS_B,
                                     preferred_element_type=F32)

    def accumulate(act_s, half):
        acc_s[...] += jnp.dot(vt_ref[:, half * te:(half + 1) * te], act_s[...],
                              preferred_element_type=F32)

    def activations(hid_s, act_s, group, row0, after):
        return _peer_tile_activations(hid_s, act_s, group, row0, r2_ref, p2_ref, n_ref, p1_ref,
                                      after)

    @pl.when(s == 0)
    def _():
        acc_s[...] = jnp.zeros(acc_s.shape, F32)
        x = x_ref[...]
        hidden(hid0_s, 0, x)
        hidden(hid1_s, 1, x)
        activations(hid0_s, act0_s, 0, 0, no_dep)

    @pl.when((s > 0) & (s < last))
    def _():
        x = x_ref[...]
        after = activations(hid1_s, act1_s, s - 1, rows_per_tile, no_dep)
        accumulate(act0_s, 0)
        hidden(hid0_s, 0, x)
        hidden(hid1_s, 1, x)
        activations(hid0_s, act0_s, s, 0, after)
        accumulate(act1_s, 1)

    @pl.when(s == last)
    def _():
        activations(hid1_s, act1_s, s - 1, rows_per_tile, no_dep)
        accumulate(act0_s, 0)
        accumulate(act1_s, 1)
        o_ref[...] = acc_s[...].T


def _peer_dense(xb, u_b, v_b, r2, p2, n_rows, p1_rows, tb=512):
    t, dm = xb.shape
    nh, nk = PEER_HEADS, PEER_NKEYS
    ne = u_b.shape[0]
    te = SUBLANES * nk // 2
    n_pairs = ne // (2 * te)
    assert n_pairs == _E1_GROUPS
    vt_b = v_b.reshape(n_pairs, 2 * te, dm).transpose(0, 2, 1)
    e2_spec = pl.BlockSpec((nh, nk // _BF16_PER_WORD, tb), lambda i, s: (0, 0, i))
    e1_spec = pl.BlockSpec((_E1_GROUPS, nh, SUBLANES, tb), lambda i, s: (0, 0, 0, i))
    return pl.pallas_call(
        _peer_dense_kernel,
        out_shape=jax.ShapeDtypeStruct((t, dm), F32),
        grid=(t // tb, n_pairs + 1),
        in_specs=[pl.BlockSpec((tb, dm), lambda i, s: (i, 0)),
                  pl.BlockSpec((2 * te, dm), lambda i, s: (jnp.minimum(s, n_pairs - 1), 0)),
                  pl.BlockSpec((None, dm, 2 * te), lambda i, s: (jnp.maximum(s - 1, 0), 0, 0)),
                  e2_spec, e2_spec, e1_spec, e1_spec],
        out_specs=pl.BlockSpec((tb, dm), lambda i, s: (i, 0)),
        scratch_shapes=[pltpu.VMEM((dm, tb), F32),
                        pltpu.VMEM((te, tb), F32), pltpu.VMEM((te, tb), F32),
                        pltpu.VMEM((te, tb), BF16), pltpu.VMEM((te, tb), BF16)],
        compiler_params=_params("parallel", "arbitrary"),
        name="peer_dense",
    )(xb, u_b, vt_b, r2, p2, n_rows, p1_rows)


def _peer(xb, w_q_b, keys_b, u_b, v_b):
    qb = _matmul(xb, w_q_b, BF16)
    r2, p2, n_rows, p1_rows = _peer_topk(qb, keys_b)
    return _peer_dense(xb, u_b, v_b, r2, p2, n_rows, p1_rows)


def _rope_tables(positions):
    half = ATT_HEAD_DIM // 2
    inv_freq = jnp.power(ROPE_THETA, -2.0 * jnp.arange(half, dtype=F32) / ATT_HEAD_DIM)
    ang = positions.astype(F32).reshape(-1, 1) * inv_freq
    cos, sin = jnp.cos(ang), jnp.sin(ang)
    return jnp.concatenate([cos, cos], axis=-1), jnp.concatenate([-sin, sin], axis=-1)


def kernel(x, p, positions, even_w_in, even_conv_w, even_w_out, odd_w_in, odd_gate_bias,
           odd_w_out, peer_w_q, peer_sub_keys, peer_u, peer_v, ple_w_proj, ple_w_gate,
           ln_gain, ln_bias):
    batch, seq, dm = x.shape
    t = batch * seq
    assert dm == D_MODEL and seq % ATT_TOKENS == 0 and seq % MLSTM_CHUNK == 0
    xf = x.reshape(t, dm)
    xb = xf.astype(BF16)
    cosf, sinf = _rope_tables(positions)

    for layer in range(DEPTH):
        j = layer // 2
        if layer % 2 == 0:
            h = _matmul(xb, even_w_in[j].astype(BF16), F32)
            y_conv = _conv_gate(h, even_conv_w[j], seq)
            y_att = _dilated_attention(h, cosf, sinf, batch, seq)
            mixed = [y_conv, y_att]
            w_out = even_w_out[j]
        else:
            w_in = odd_w_in[j]
            h = _matmul(xb, w_in[:, :ODD_MAIN].astype(BF16), F32)
            n_gates = 2 * MLSTM_HEADS
            w_gates = jnp.pad(w_in[:, ODD_MAIN:], ((0, 0), (0, LANES - n_gates))).astype(BF16)
            gates = _matmul(xb, w_gates, F32)
            bias_row = jnp.pad(odd_gate_bias[j].reshape(1, n_gates), ((0, 0), (0, LANES - n_gates)))
            mixed = [_mlstm(h, gates, bias_row, batch, seq)]
            w_out = odd_w_out[j]
        xf, xb = _matmul_residual_ln(mixed, w_out.astype(BF16), xf,
                                     ln_gain[layer, 0], ln_bias[layer, 0])
        ffn = _peer(xb, peer_w_q[layer].astype(BF16), peer_sub_keys[layer].astype(BF16),
                    peer_u[layer].astype(BF16), peer_v[layer].astype(BF16))
        xf, xb = _ple_residual_ln(xf, xb, ffn, p[layer].reshape(t, PLE_DIM).astype(BF16),
                                  ple_w_gate[layer].astype(BF16), ple_w_proj[layer].astype(BF16),
                                  ln_gain[layer, 1], ln_bias[layer, 1])
    return xf.reshape(batch, seq, dm)
```

```python
import functools
import math

import jax
import jax.numpy as jnp
from jax import lax
from jax.experimental import pallas as pl
from jax.experimental.pallas import tpu as pltpu

D_MODEL = 2048
DEPTH = 4
CONV_CH = D_MODEL // 2
CONV_WIDTH = 3
ATT_HEADS = 8
ATT_HEAD_DIM = 128
ATT_WIDTH = ATT_HEADS * ATT_HEAD_DIM
DILATED_PAIRS = ((128, 1), (512, 4), (2048, 16))
ROPE_THETA = 10000.0
EVEN_IN = 3 * CONV_CH + 3 * ATT_WIDTH
MLSTM_HEADS = 8
MLSTM_QK_DIM = 128
MLSTM_V_DIM = D_MODEL // MLSTM_HEADS
MLSTM_CHUNK = 128
ODD_MAIN = 2 * MLSTM_HEADS * MLSTM_QK_DIM + 2 * MLSTM_HEADS * MLSTM_V_DIM
PEER_HEADS = 8
PEER_NKEYS = 128
PEER_QDIM = 256
PEER_TOPK = 16
PEER_EXPERTS = PEER_NKEYS * PEER_NKEYS
PLE_DIM = 256
ALPHA = (2.0 * DEPTH) ** 0.25
LN_EPS = 1e-5

LANES = 128
SUBLANES = 8
VMEM_LIMIT_BYTES = 56 * 1024 * 1024

ATT_BLOCK = 128
ATT_SPAN = 128
ATT_TOKENS = ATT_BLOCK * max(d for _, d in DILATED_PAIRS)

NEG_INF = float("-inf")
F32 = jnp.float32
BF16 = jnp.bfloat16

_TRANS_B = (((1,), (1,)), ((), ()))


def _params(*semantics):
    return pltpu.CompilerParams(dimension_semantics=semantics,
                                vmem_limit_bytes=VMEM_LIMIT_BYTES)


def _layer_norm_rows(y, gain, bias):
    mu = jnp.mean(y, axis=-1, keepdims=True)
    yc = y - mu
    var = jnp.mean(yc * yc, axis=-1, keepdims=True)
    return yc * lax.rsqrt(var + LN_EPS) * gain + bias


def _mm_kernel(a_ref, w_ref, o_ref):
    o_ref[...] = jnp.dot(a_ref[...], w_ref[...],
                         preferred_element_type=F32).astype(o_ref.dtype)


def _matmul(a, w, out_dtype, tm=512, tn=None):
    m, k = a.shape
    n = w.shape[1]
    tn = min(n, 2048) if tn is None else tn
    return pl.pallas_call(
        _mm_kernel,
        out_shape=jax.ShapeDtypeStruct((m, n), out_dtype),
        grid=(n // tn, m // tm),
        in_specs=[pl.BlockSpec((tm, k), lambda j, i: (i, 0)),
                  pl.BlockSpec((k, tn), lambda j, i: (0, j))],
        out_specs=pl.BlockSpec((tm, tn), lambda j, i: (i, j)),
        compiler_params=_params("parallel", "parallel"),
        name="matmul",
    )(a, w)


def _mm_ln_kernel(*refs, n_a):
    a_refs = refs[:n_a]
    w_ref, res_ref, g_ref, b_ref, o_ref, ob_ref = refs[n_a:]
    acc = None
    off = 0
    for a_ref in a_refs:
        k = a_ref.shape[1]
        part = jnp.dot(a_ref[...], w_ref[off:off + k, :], preferred_element_type=F32)
        acc = part if acc is None else acc + part
        off += k
    y = ALPHA * res_ref[...] + acc
    out = _layer_norm_rows(y, g_ref[...], b_ref[...])
    o_ref[...] = out
    ob_ref[...] = out.astype(BF16)


def _matmul_residual_ln(a_list, w, res, gain, bias, tm=256):
    m = res.shape[0]
    n = w.shape[1]
    in_specs = [pl.BlockSpec((tm, a.shape[1]), lambda i: (i, 0)) for a in a_list]
    in_specs += [pl.BlockSpec(w.shape, lambda i: (0, 0)),
                 pl.BlockSpec((tm, n), lambda i: (i, 0)),
                 pl.BlockSpec((1, n), lambda i: (0, 0)),
                 pl.BlockSpec((1, n), lambda i: (0, 0))]
    return pl.pallas_call(
        functools.partial(_mm_ln_kernel, n_a=len(a_list)),
        out_shape=(jax.ShapeDtypeStruct((m, n), F32), jax.ShapeDtypeStruct((m, n), BF16)),
        grid=(m // tm,),
        in_specs=in_specs,
        out_specs=(pl.BlockSpec((tm, n), lambda i: (i, 0)),
                   pl.BlockSpec((tm, n), lambda i: (i, 0))),
        compiler_params=_params("parallel"),
        name="matmul_residual_ln",
    )(*a_list, w, res, gain.reshape(1, n), bias.reshape(1, n))


def _ple_ln_kernel(x_ref, xb_ref, ffn_ref, p_ref, wg_ref, wp_ref, g_ref, b_ref, o_ref, ob_ref):
    gate = jax.nn.sigmoid(jnp.dot(xb_ref[...], wg_ref[...], preferred_element_type=F32))
    proj = jnp.dot(p_ref[...].astype(BF16), wp_ref[...], preferred_element_type=F32)
    y = ALPHA * x_ref[...] + ffn_ref[...] + gate * proj
    out = _layer_norm_rows(y, g_ref[...], b_ref[...])
    o_ref[...] = out
    ob_ref[...] = out.astype(BF16)


def _ple_residual_ln(x, xb, ffn, p, wg, wp, gain, bias, tm=256):
    m, n = x.shape
    row = lambda i: (i, 0)
    const = lambda i: (0, 0)
    return pl.pallas_call(
        _ple_ln_kernel,
        out_shape=(jax.ShapeDtypeStruct((m, n), F32), jax.ShapeDtypeStruct((m, n), BF16)),
        grid=(m // tm,),
        in_specs=[pl.BlockSpec((tm, n), row), pl.BlockSpec((tm, n), row),
                  pl.BlockSpec((tm, n), row), pl.BlockSpec((tm, p.shape[1]), row),
                  pl.BlockSpec(wg.shape, const), pl.BlockSpec(wp.shape, const),
                  pl.BlockSpec((1, n), const), pl.BlockSpec((1, n), const)],
        out_specs=(pl.BlockSpec((tm, n), row), pl.BlockSpec((tm, n), row)),
        compiler_params=_params("parallel"),
        name="ple_residual_ln",
    )(x, xb, ffn, p, wg, wp, gain.reshape(1, n), bias.reshape(1, n))


def _conv_gate_kernel(gb_ref, gc_ref, xa_ref, w_ref, o_ref, buf, *, steps_per_seq):
    tm = gb_ref.shape[0]
    halo = SUBLANES

    @pl.when(pl.program_id(0) % steps_per_seq == 0)
    def _():
        buf[0:halo, :] = jnp.zeros((halo, buf.shape[1]), F32)

    u = gc_ref[...] * xa_ref[...]
    buf[halo:halo + tm, :] = u
    w = w_ref[...]
    conv = (w[0:1, :] * buf[halo - 2:halo - 2 + tm, :]
            + w[1:2, :] * buf[halo - 1:halo - 1 + tm, :]
            + w[2:3, :] * u)
    o_ref[...] = (gb_ref[...] * conv).astype(o_ref.dtype)
    buf[0:halo, :] = buf[tm:tm + halo, :]


def _conv_gate(h, conv_w, seq, tm=512):
    t = h.shape[0]
    ch = CONV_CH
    return pl.pallas_call(
        functools.partial(_conv_gate_kernel, steps_per_seq=seq // tm),
        out_shape=jax.ShapeDtypeStruct((t, ch), BF16),
        grid=(t // tm,),
        in_specs=[pl.BlockSpec((tm, ch), lambda i: (i, 0)),
                  pl.BlockSpec((tm, ch), lambda i: (i, 1)),
                  pl.BlockSpec((tm, ch), lambda i: (i, 2)),
                  pl.BlockSpec((CONV_WIDTH, ch), lambda i: (0, 0))],
        out_specs=pl.BlockSpec((tm, ch), lambda i: (i, 0)),
        scratch_shapes=[pltpu.VMEM((tm + SUBLANES, ch), F32)],
        compiler_params=_params("arbitrary"),
        name="conv_gate",
    )(h, h, h, conv_w)


def _rope(t, cosf, sinf):
    return t * cosf + pltpu.roll(t, ATT_HEAD_DIM // 2, 1) * sinf


def _attn_kernel(q_ref, kc_ref, kp_ref, vc_ref, vp_ref, cc_ref, sc_ref, cp_ref, sp_ref,
                 o_ref, q_s, k_s, v_s, acc_s, m_s, l_s):
    j = pl.program_id(2)
    bl = ATT_TOKENS
    blk = ATT_BLOCK
    cc = cc_ref[...]
    sc = sc_ref[...]
    q_s[...] = _rope(q_ref[...], cc, sc)
    k_s[bl:2 * bl, :] = _rope(kc_ref[...], cc, sc)
    k_s[0:bl, :] = _rope(kp_ref[...], cp_ref[...], sp_ref[...])
    v_s[bl:2 * bl, :] = vc_ref[...]
    v_s[0:bl, :] = vp_ref[...]

    qi = lax.broadcasted_iota(jnp.int32, (blk, 2 * blk), 0)
    kj = lax.broadcasted_iota(jnp.int32, (blk, 2 * blk), 1)
    dist = qi + blk - kj
    band = (dist >= 0) & (dist <= ATT_SPAN)
    band_first = band & (kj >= jnp.where(j > 0, 0, blk))
    scale = ATT_HEAD_DIM ** -0.5

    for branch, (_, d) in enumerate(DILATED_PAIRS):
        for g in range(bl // (blk * d)):
            for r in range(d):
                q_rows = pl.ds(g * blk * d + r, blk, stride=d) if d > 1 else pl.ds(g * blk, blk)
                k_start = bl + (g - 1) * blk * d + r
                k_rows = pl.ds(k_start, 2 * blk, stride=d) if d > 1 else pl.ds(k_start, 2 * blk)
                qs = q_s[q_rows, :].astype(BF16)
                ks = k_s[k_rows, :].astype(BF16)
                vs = v_s[k_rows, :].astype(BF16)
                s = lax.dot_general(qs, ks, _TRANS_B, preferred_element_type=F32) * scale
                s = jnp.where(band_first if g == 0 else band, s, NEG_INF)
                m_b = jnp.max(s, axis=-1, keepdims=True)
                e = jnp.exp(s - m_b)
                l_b = jnp.sum(e, axis=-1, keepdims=True)
                o_b = jnp.dot(e.astype(BF16), vs, preferred_element_type=F32)
                if branch == 0:
                    acc_s[q_rows, :] = o_b
                    m_s[q_rows, :] = jnp.broadcast_to(m_b, (blk, LANES))
                    l_s[q_rows, :] = jnp.broadcast_to(l_b, (blk, LANES))
                else:
                    m_old = m_s[q_rows, :]
                    m_new = jnp.maximum(m_old, m_b)
                    a_old = jnp.exp(m_old - m_new)
                    a_new = jnp.exp(m_b - m_new)
                    acc_s[q_rows, :] = a_old * acc_s[q_rows, :] + a_new * o_b
                    l_s[q_rows, :] = a_old * l_s[q_rows, :] + a_new * l_b
                    m_s[q_rows, :] = m_new

    o_ref[...] = (acc_s[...] / l_s[...]).astype(o_ref.dtype)


def _dilated_attention(h, cosf, sinf, batch, seq):
    t = h.shape[0]
    bl = ATT_TOKENS
    nj = seq // bl
    hd = ATT_HEAD_DIM
    q_col = 3 * CONV_CH // hd
    k_col = q_col + ATT_HEADS
    v_col = k_col + ATT_HEADS

    def cur(col):
        return lambda b, hh, j: (b * nj + j, col + hh)

    def prev(col):
        return lambda b, hh, j: (b * nj + jnp.maximum(j - 1, 0), col + hh)

    rows_cur = lambda b, hh, j: (b * nj + j, 0)
    rows_prev = lambda b, hh, j: (b * nj + jnp.maximum(j - 1, 0), 0)
    blk = (bl, hd)
    return pl.pallas_call(
        _attn_kernel,
        out_shape=jax.ShapeDtypeStruct((t, ATT_WIDTH), BF16),
        grid=(batch, ATT_HEADS, nj),
        in_specs=[pl.BlockSpec(blk, cur(q_col)),
                  pl.BlockSpec(blk, cur(k_col)), pl.BlockSpec(blk, prev(k_col)),
                  pl.BlockSpec(blk, cur(v_col)), pl.BlockSpec(blk, prev(v_col)),
                  pl.BlockSpec(blk, rows_cur), pl.BlockSpec(blk, rows_cur),
                  pl.BlockSpec(blk, rows_prev), pl.BlockSpec(blk, rows_prev)],
        out_specs=pl.BlockSpec(blk, lambda b, hh, j: (b * nj + j, hh)),
        scratch_shapes=[pltpu.VMEM((bl, hd), F32), pltpu.VMEM((2 * bl, hd), F32),
                        pltpu.VMEM((2 * bl, hd), F32), pltpu.VMEM((bl, hd), F32),
                        pltpu.VMEM((bl, LANES), F32), pltpu.VMEM((bl, LANES), F32)],
        compiler_params=_params("parallel", "parallel", "parallel"),
        name="dilated_attention",
    )(h, h, h, h, h, cosf, sinf, cosf, sinf)


def _log_sigmoid(x):
    return jnp.minimum(x, 0.0) - jnp.log(1.0 + jnp.exp(-jnp.abs(x)))


def _mlstm_kernel(q_ref, k_ref, v_ref, og_ref, g_ref, bias_ref, o_ref, ct_s, n_s, m_s):
    nh, dk, dv, cl = MLSTM_HEADS, MLSTM_QK_DIM, MLSTM_V_DIM, MLSTM_CHUNK
    n_seq = q_ref.shape[0]

    @pl.when(pl.program_id(0) == 0)
    def _():
        ct_s[...] = jnp.zeros(ct_s.shape, F32)
        n_s[...] = jnp.zeros(n_s.shape, F32)
        m_s[...] = jnp.zeros(m_s.shape, F32)

    row = lax.broadcasted_iota(jnp.int32, (cl, cl), 0)
    col = lax.broadcasted_iota(jnp.int32, (cl, cl), 1)
    causal = col <= row
    ones = jnp.ones((cl, LANES), BF16)

    for b in range(n_seq):
        gates = g_ref[b] + bias_ref[...]
        gates_t = gates.T
        for hd in range(nh):
            st = b * nh + hd
            ig_col = gates[:, hd:hd + 1]
            ig_row = gates_t[hd:hd + 1, :]
            lf_col = _log_sigmoid(gates[:, nh + hd:nh + hd + 1])
            lf_row = _log_sigmoid(gates_t[nh + hd:nh + hd + 1, :])
            bcum_col = jnp.sum(jnp.where(causal, lf_row, 0.0), axis=1, keepdims=True)
            bcum_row = jnp.sum(jnp.where(row <= col, lf_col, 0.0), axis=0, keepdims=True)
            btot = bcum_col[cl - 1:cl, :]

            qb = q_ref[b, :, hd * dk:(hd + 1) * dk].astype(BF16)
            kf = k_ref[b, :, hd * dk:(hd + 1) * dk] * (dk ** -0.5)
            vf = v_ref[b, :, hd * dv:(hd + 1) * dv]
            kb = kf.astype(BF16)
            ct_prev = ct_s[st]
            n_prev = n_s[st]
            m_prev = m_s[st][:, 0:1]

            dmat = jnp.where(causal, bcum_col - bcum_row + ig_row, NEG_INF)
            g_inter = bcum_col + m_prev
            m_t = jnp.maximum(g_inter, jnp.max(dmat, axis=1, keepdims=True))
            qk = lax.dot_general(qb, kb, _TRANS_B, preferred_element_type=F32)
            sc = (qk * jnp.exp(dmat - m_t)).astype(BF16)
            a_inter = jnp.exp(g_inter - m_t)
            num = (a_inter * jnp.dot(qb, ct_prev.astype(BF16), preferred_element_type=F32)
                   + jnp.dot(sc, vf.astype(BF16), preferred_element_type=F32))
            n_rows = jnp.broadcast_to(n_prev, (LANES, dk)).astype(BF16)
            den = (a_inter * lax.dot_general(qb, n_rows, _TRANS_B, preferred_element_type=F32)
                   + jnp.dot(sc, ones, preferred_element_type=F32))
            scale = 1.0 / jnp.maximum(jnp.abs(den), jnp.exp(-m_t))
            hh = num * jnp.tile(scale, (1, dv // LANES))
            o_ref[b, :, hd * dv:(hd + 1) * dv] = (
                jax.nn.sigmoid(og_ref[b, :, hd * dv:(hd + 1) * dv]) * hh).astype(o_ref.dtype)

            w_loc = btot - bcum_col + ig_col
            m_loc = jnp.max(w_loc, axis=0, keepdims=True)
            a_loc = jnp.exp(w_loc - m_loc)
            ct_loc = jnp.dot(kf.T.astype(BF16), (vf * a_loc).astype(BF16),
                             preferred_element_type=F32)
            n_loc = jnp.sum(a_loc * kf, axis=0, keepdims=True)
            m_new = jnp.maximum(btot + m_prev, m_loc)
            a_old = jnp.exp(btot + m_prev - m_new)
            a_new = jnp.exp(m_loc - m_new)
            ct_s[st] = a_old * ct_prev + a_new * ct_loc
            n_s[st] = a_old * n_prev + a_new * n_loc
            m_s[st] = jnp.broadcast_to(m_new, (1, LANES))


def _mlstm(h, gates, gate_bias_row, batch, seq):
    t = h.shape[0]
    nh, dk, dv, cl = MLSTM_HEADS, MLSTM_QK_DIM, MLSTM_V_DIM, MLSTM_CHUNK
    h3 = h.reshape(batch, seq, h.shape[1])
    g3 = gates.reshape(batch, seq, LANES)
    cols = lambda col: (lambda c: (0, c, col))
    out = pl.pallas_call(
        _mlstm_kernel,
        out_shape=jax.ShapeDtypeStruct((batch, seq, nh * dv), BF16),
        grid=(seq // cl,),
        in_specs=[pl.BlockSpec((batch, cl, nh * dk), cols(0)),
                  pl.BlockSpec((batch, cl, nh * dk), cols(1)),
                  pl.BlockSpec((batch, cl, nh * dv), cols(1)),
                  pl.BlockSpec((batch, cl, nh * dv), cols(2)),
                  pl.BlockSpec((batch, cl, LANES), cols(0)),
                  pl.BlockSpec((1, LANES), lambda c: (0, 0))],
        out_specs=pl.BlockSpec((batch, cl, nh * dv), cols(0)),
        scratch_shapes=[pltpu.VMEM((batch * nh, dk, dv), F32),
                        pltpu.VMEM((batch * nh, 1, dk), F32),
                        pltpu.VMEM((batch * nh, 1, LANES), F32)],
        compiler_params=_params("arbitrary"),
        name="mlstm",
    )(h3, h3, h3, h3, g3, gate_bias_row)
    return out.reshape(t, nh * dv)


_E1_GROUPS = PEER_NKEYS // SUBLANES
_E1_ROWS_PER_PASS = 2
_BF16_PER_WORD = 2


def _pack_bf16(x):
    return pltpu.bitcast(x.astype(BF16), jnp.uint32)


def _unpack_bf16(words):
    return pltpu.bitcast(words, BF16)


def _bf16_rows(row, n_rows):
    tile_rows = SUBLANES * _BF16_PER_WORD
    return jnp.tile(jnp.broadcast_to(row, (tile_rows, row.shape[1])).astype(BF16),
                    (n_rows // tile_rows, 1))


def _sorting_network(n):
    pairs = []
    p = 1
    while p < n:
        k = p
        while k >= 1:
            for j in range(k % p, n - k, 2 * k):
                for i in range(min(k, n - j - k)):
                    if (i + j) // (2 * p) == (i + j + k) // (2 * p):
                        pairs.append((i + j, i + j + k))
            k //= 2
        p *= 2
    return pairs


def _pop_sorted_columns(levels, k, extra=None, on_pop=None):
    levels = list(levels)
    depth = len(levels)
    vals = []
    for t in range(k):
        top = levels[0] if extra is None else jnp.maximum(levels[0], extra)
        mx = jnp.max(top, axis=0, keepdims=True)
        vals.append(mx)
        if on_pop is not None:
            on_pop(mx)
        hit = levels[0] == mx
        if extra is not None:
            extra = jnp.where(extra == mx, NEG_INF, extra)
        for lv in range(min(depth, k - t)):
            below = levels[lv + 1] if lv + 1 < depth else NEG_INF
            levels[lv] = jnp.where(hit, below, levels[lv])
    return vals


def _count_passing(rows, passes):
    def probe(bits, index, step):
        if not bits:
            return rows[index + step - 1]
        (mask, weight), rest = bits[0], bits[1:]
        return jnp.where(mask, probe(rest, index + weight, step), probe(rest, index, step))

    bits = []
    count = jnp.where(passes(rows[-1]), 1.0, 0.0)
    step = len(rows) // 2
    while step >= 1:
        mask = passes(probe(bits, 0, step))
        bits.append((mask, step))
        count = count + jnp.where(mask, float(step), 0.0)
        step //= 2
    return count


def _top_values(s, k):
    levels = [s[i * SUBLANES:(i + 1) * SUBLANES] for i in range(s.shape[0] // SUBLANES)]
    for i, j in _sorting_network(len(levels)):
        levels[i], levels[j] = jnp.maximum(levels[i], levels[j]), jnp.minimum(levels[i], levels[j])
    return _pop_sorted_columns(levels, k)


def _peer_topk_kernel(q_ref, keys_ref, r2_ref, p2_ref, n_ref, p1_ref, s1_s, s2_s):
    kk = PEER_TOPK
    assert kk == 2 * SUBLANES
    half = PEER_QDIM // 2
    tb = q_ref.shape[0]
    q = q_ref[...]
    s1_s[...] = lax.dot_general(keys_ref[0, 0], q[:, :half], _TRANS_B, preferred_element_type=F32)
    s2_s[...] = lax.dot_general(keys_ref[0, 1], q[:, half:], _TRANS_B, preferred_element_type=F32)

    def chunk(c):
        lanes = pl.ds(pl.multiple_of(c * LANES, LANES), LANES)
        s1 = s1_s[:, lanes]
        s2 = s2_s[:, lanes]
        v1 = _top_values(s1, kk)
        v2 = _top_values(s2, kk)
        v2_lo = jnp.concatenate(v2[:SUBLANES], axis=0)
        v2_hi = jnp.concatenate(v2[SUBLANES:], axis=0)
        m = v1[0] + v2[0]
        exps = []
        sums = _pop_sorted_columns([v1[a] + v2_lo for a in range(kk)], kk, extra=v1[0] + v2_hi,
                                   on_pop=lambda mx: exps.append(jnp.exp(mx - m)))
        thr = sums[-1]
        den = functools.reduce(lambda a, b: a + b, exps)
        p2_ref[0, :, lanes] = _pack_bf16(jnp.exp(s2 - v2[0]) / den)
        rank2 = _count_passing(v2, lambda t: t > s2)
        n = _count_passing(v2, lambda t: s1 + t >= thr)
        r2_ref[0, :, lanes] = _pack_bf16(rank2)
        p1 = jnp.exp(s1 - v1[0])
        for g in range(_E1_GROUPS):
            rows = slice(g * SUBLANES, (g + 1) * SUBLANES)
            n_ref[g, 0, :, lanes] = n[rows]
            p1_ref[g, 0, :, lanes] = p1[rows]

    def chunk_pair(i, carry):
        chunk(2 * i)
        chunk(2 * i + 1)
        return carry

    lax.fori_loop(0, tb // (2 * LANES), chunk_pair, 0)


def _peer_topk(qb, keys_b, tb=512):
    t = qb.shape[0]
    nh, nk = PEER_HEADS, PEER_NKEYS
    e2_spec = pl.BlockSpec((1, nk // _BF16_PER_WORD, tb), lambda i, h: (h, 0, i))
    e1_spec = pl.BlockSpec((_E1_GROUPS, 1, SUBLANES, tb), lambda i, h: (0, h, 0, i))
    return pl.pallas_call(
        _peer_topk_kernel,
        out_shape=(jax.ShapeDtypeStruct((nh, nk // _BF16_PER_WORD, t), jnp.uint32),
                   jax.ShapeDtypeStruct((nh, nk // _BF16_PER_WORD, t), jnp.uint32),
                   jax.ShapeDtypeStruct((_E1_GROUPS, nh, SUBLANES, t), F32),
                   jax.ShapeDtypeStruct((_E1_GROUPS, nh, SUBLANES, t), F32)),
        grid=(t // tb, nh),
        in_specs=[pl.BlockSpec((tb, PEER_QDIM), lambda i, h: (i, h)),
                  pl.BlockSpec((1, 2, nk, PEER_QDIM // 2), lambda i, h: (h, 0, 0, 0))],
        out_specs=(e2_spec, e2_spec, e1_spec, e1_spec),
        scratch_shapes=[pltpu.VMEM((nk, tb), F32), pltpu.VMEM((nk, tb), F32)],
        compiler_params=_params("parallel", "parallel"),
        name="peer_topk",
    )(qb, keys_b)


def _gelu_exact(x):
    return 0.5 * x * (1.0 + lax.erf(x * (2.0 ** -0.5)))


def _zero_after(v):
    bits = pltpu.bitcast(v, jnp.uint32)
    return pltpu.bitcast((bits >> 16) >> 16, F32)[0:1, :]


def _peer_tile_activations(hid_s, act_s, group, row0, r2_ref, p2_ref, n_ref, p1_ref, after):
    nh, nk = PEER_HEADS, PEER_NKEYS
    te, tb = hid_s.shape
    zero = jnp.zeros((nk, LANES), BF16)
    for c in range(tb // LANES):
        lanes = slice(c * LANES, (c + 1) * LANES)
        for le1_0 in range(0, te // nk, _E1_ROWS_PER_PASS):
            le1s = range(le1_0, le1_0 + _E1_ROWS_PER_PASS)
            w = [zero for _ in le1s]
            for h in range(nh):
                rank2 = _unpack_bf16(r2_ref[h, :, lanes])
                p2 = _unpack_bf16(p2_ref[h, :, lanes])
                for j, le1 in enumerate(le1s):
                    r = row0 + le1
                    n_rows = _bf16_rows(n_ref[group, h, r:r + 1, lanes] + after, nk)
                    p1_rows = _bf16_rows(p1_ref[group, h, r:r + 1, lanes], nk)
                    w[j] = w[j] + jnp.where(rank2 < n_rows, p2 * p1_rows, zero)
            for j, le1 in enumerate(le1s):
                rows = slice(le1 * nk, (le1 + 1) * nk)
                act = w[j] * _gelu_exact(hid_s[rows, lanes]).astype(BF16)
                act_s[rows, lanes] = act
            after = _zero_after(act[0:2 * SUBLANES, :])
    return after


def _peer_dense_kernel(x_ref, u_ref, vt_ref, r2_ref, p2_ref, n_ref, p1_ref, o_ref,
                       acc_s, hid0_s, hid1_s, act0_s, act1_s):
    te = hid0_s.shape[0]
    rows_per_tile = te // PEER_NKEYS
    s = pl.program_id(1)
    last = pl.num_programs(1) - 1
    no_dep = jnp.zeros((1, LANES), F32)

    def hidden(hid_s, half, x):
        hid_s[...] = lax.dot_general(u_ref[half * te:(half + 1) * te, :], x, _TRANS_B,
                                     preferred_element_type=F32)

    def accumulate(act_s, half):
        acc_s[...] += jnp.dot(vt_ref[:, half * te:(half + 1) * te], act_s[...],
                              preferred_element_type=F32)

    def activations(hid_s, act_s, group, row0, after):
        return _peer_tile_activations(hid_s, act_s, group, row0, r2_ref, p2_ref, n_ref, p1_ref,
                                      after)

    @pl.when(s == 0)
    def _():
        acc_s[...] = jnp.zeros(acc_s.shape, F32)
        x = x_ref[...]
        hidden(hid0_s, 0, x)
        hidden(hid1_s, 1, x)
        activations(hid0_s, act0_s, 0, 0, no_dep)

    @pl.when((s > 0) & (s < last))
    def _():
        x = x_ref[...]
        after = activations(hid1_s, act1_s, s - 1, rows_per_tile, no_dep)
        accumulate(act0_s, 0)
        hidden(hid0_s, 0, x)
        hidden(hid1_s, 1, x)
        activations(hid0_s, act0_s, s, 0, after)
        accumulate(act1_s, 1)

    @pl.when(s == last)
    def _():
        activations(hid1_s, act1_s, s - 1, rows_per_tile, no_dep)
        accumulate(act0_s, 0)
        accumulate(act1_s, 1)
        o_ref[...] = acc_s[...].T


def _peer_dense(xb, u_b, v_b, r2, p2, n_rows, p1_rows, tb=512):
    t, dm = xb.shape
    nh, nk = PEER_HEADS, PEER_NKEYS
    ne = u_b.shape[0]
    te = SUBLANES * nk // 2
    n_pairs = ne // (2 * te)
    assert n_pairs == _E1_GROUPS
    vt_b = v_b.reshape(n_pairs, 2 * te, dm).transpose(0, 2, 1)
    e2_spec = pl.BlockSpec((nh, nk // _BF16_PER_WORD, tb), lambda i, s: (0, 0, i))
    e1_spec = pl.BlockSpec((_E1_GROUPS, nh, SUBLANES, tb), lambda i, s: (0, 0, 0, i))
    return pl.pallas_call(
        _peer_dense_kernel,
        out_shape=jax.ShapeDtypeStruct((t, dm), F32),
        grid=(t // tb, n_pairs + 1),
        in_specs=[pl.BlockSpec((tb, dm), lambda i, s: (i, 0)),
                  pl.BlockSpec((2 * te, dm), lambda i, s: (jnp.minimum(s, n_pairs - 1), 0)),
                  pl.BlockSpec((None, dm, 2 * te), lambda i, s: (jnp.maximum(s - 1, 0), 0, 0)),
                  e2_spec, e2_spec, e1_spec, e1_spec],
        out_specs=pl.BlockSpec((tb, dm), lambda i, s: (i, 0)),
        scratch_shapes=[pltpu.VMEM((dm, tb), F32),
                        pltpu.VMEM((te, tb), F32), pltpu.VMEM((te, tb), F32),
                        pltpu.VMEM((te, tb), BF16), pltpu.VMEM((te, tb), BF16)],
        compiler_params=_params("parallel", "arbitrary"),
        name="peer_dense",
    )(xb, u_b, vt_b, r2, p2, n_rows, p1_rows)


def _peer(xb, w_q_b, keys_b, u_b, v_b):
    qb = _matmul(xb, w_q_b, BF16)
    r2, p2, n_rows, p1_rows = _peer_topk(qb, keys_b)
    return _peer_dense(xb, u_b, v_b, r2, p2, n_rows, p1_rows)


def _rope_tables(positions):
    half = ATT_HEAD_DIM // 2
    inv_freq = jnp.power(ROPE_THETA, -2.0 * jnp.arange(half, dtype=F32) / ATT_HEAD_DIM)
    ang = positions.astype(F32).reshape(-1, 1) * inv_freq
    cos, sin = jnp.cos(ang), jnp.sin(ang)
    return jnp.concatenate([cos, cos], axis=-1), jnp.concatenate([-sin, sin], axis=-1)


def kernel(x, p, positions, even_w_in, even_conv_w, even_w_out, odd_w_in, odd_gate_bias,
           odd_w_out, peer_w_q, peer_sub_keys, peer_u, peer_v, ple_w_proj, ple_w_gate,
           ln_gain, ln_bias):
    batch, seq, dm = x.shape
    t = batch * seq
    assert dm == D_MODEL and seq % ATT_TOKENS == 0 and seq % MLSTM_CHUNK == 0
    xf = x.reshape(t, dm)
    xb = xf.astype(BF16)
    cosf, sinf = _rope_tables(positions)

    for layer in range(DEPTH):
        j = layer // 2
        if layer % 2 == 0:
            h = _matmul(xb, even_w_in[j].astype(BF16), F32)
            y_conv = _conv_gate(h, even_conv_w[j], seq)
            y_att = _dilated_attention(h, cosf, sinf, batch, seq)
            mixed = [y_conv, y_att]
            w_out = even_w_out[j]
        else:
            w_in = odd_w_in[j]
            h = _matmul(xb, w_in[:, :ODD_MAIN].astype(BF16), F32)
            n_gates = 2 * MLSTM_HEADS
            w_gates = jnp.pad(w_in[:, ODD_MAIN:], ((0, 0), (0, LANES - n_gates))).astype(BF16)
            gates = _matmul(xb, w_gates, F32)
            bias_row = jnp.pad(odd_gate_bias[j].reshape(1, n_gates), ((0, 0), (0, LANES - n_gates)))
            mixed = [_mlstm(h, gates, bias_row, batch, seq)]
            w_out = odd_w_out[j]
        xf, xb = _matmul_residual_ln(mixed, w_out.astype(BF16), xf,
                                     ln_gain[layer, 0], ln_bias[layer, 0])
        ffn = _peer(xb, peer_w_q[layer].astype(BF16), peer_sub_keys[layer].astype(BF16),
                    peer_u[layer].astype(BF16), peer_v[layer].astype(BF16))
        xf, xb = _ple_residual_ln(xf, xb, ffn, p[layer].reshape(t, PLE_DIM),
                                  ple_w_gate[layer].astype(BF16), ple_w_proj[layer].astype(BF16),
                                  ln_gain[layer, 1], ln_bias[layer, 1])
    return xf.reshape(batch, seq, dm)
```

```python
import functools

import jax
import jax.numpy as jnp
from jax import lax
from jax.experimental import pallas as pl
from jax.experimental.pallas import tpu as pltpu

D_MODEL = 2048
DEPTH = 4
CONV_CH = D_MODEL // 2
CONV_WIDTH = 3
ATT_HEADS = 8
ATT_HEAD_DIM = 128
ATT_WIDTH = ATT_HEADS * ATT_HEAD_DIM
DILATED_PAIRS = ((128, 1), (512, 4), (2048, 16))
ROPE_THETA = 10000.0
MLSTM_HEADS = 8
MLSTM_QK_DIM = 128
MLSTM_V_DIM = D_MODEL // MLSTM_HEADS
MLSTM_CHUNK = 128
ODD_MAIN = 2 * MLSTM_HEADS * MLSTM_QK_DIM + 2 * MLSTM_HEADS * MLSTM_V_DIM
PEER_HEADS = 8
PEER_NKEYS = 128
PEER_QDIM = 256
PEER_TOPK = 16
PLE_DIM = 256
ALPHA = (2.0 * DEPTH) ** 0.25
LN_EPS = 1e-5

LANES = 128
SUBLANES = 8
MXU_TILE = 256
VMEM_LIMIT_BYTES = 56 * 1024 * 1024

ATT_BLOCK = 128
ATT_SPAN = 128
ATT_TOKENS = ATT_BLOCK * max(d for _, d in DILATED_PAIRS)

NEG_INF = float("-inf")
F32 = jnp.float32
BF16 = jnp.bfloat16

_TRANS_B = (((1,), (1,)), ((), ()))


def _params(*semantics):
    return pltpu.CompilerParams(dimension_semantics=semantics,
                                vmem_limit_bytes=VMEM_LIMIT_BYTES)


def _layer_norm_rows(y, gain, bias):
    mu = jnp.mean(y, axis=-1, keepdims=True)
    yc = y - mu
    var = jnp.mean(yc * yc, axis=-1, keepdims=True)
    return yc * lax.rsqrt(var + LN_EPS) * gain + bias


def _mm_kernel(a_ref, w_ref, o_ref):
    o_ref[...] = jnp.dot(a_ref[...], w_ref[...],
                         preferred_element_type=F32).astype(o_ref.dtype)


def _matmul(a, w, out_dtype, tm=512, tn=None):
    m, k = a.shape
    n = w.shape[1]
    tn = min(n, 2048) if tn is None else tn
    return pl.pallas_call(
        _mm_kernel,
        out_shape=jax.ShapeDtypeStruct((m, n), out_dtype),
        grid=(n // tn, m // tm),
        in_specs=[pl.BlockSpec((tm, k), lambda j, i: (i, 0)),
                  pl.BlockSpec((k, tn), lambda j, i: (0, j))],
        out_specs=pl.BlockSpec((tm, tn), lambda j, i: (i, j)),
        compiler_params=_params("parallel", "parallel"),
        name="matmul",
    )(a, w)


def _mm_ln_kernel(*refs, n_a):
    a_refs = refs[:n_a]
    w_ref, res_ref, g_ref, b_ref, o_ref, ob_ref = refs[n_a:]
    for r0 in range(0, res_ref.shape[0], MXU_TILE):
        rows = slice(r0, r0 + MXU_TILE)
        acc = None
        off = 0
        for a_ref in a_refs:
            k = a_ref.shape[1]
            part = jnp.dot(a_ref[rows, :], w_ref[off:off + k, :], preferred_element_type=F32)
            acc = part if acc is None else acc + part
            off += k
        y = ALPHA * res_ref[rows, :] + acc
        out = _layer_norm_rows(y, g_ref[...], b_ref[...])
        o_ref[rows, :] = out
        ob_ref[rows, :] = out.astype(BF16)


def _matmul_residual_ln(a_list, w, res, gain, bias, tm=512):
    m = res.shape[0]
    n = w.shape[1]
    in_specs = [pl.BlockSpec((tm, a.shape[1]), lambda i: (i, 0)) for a in a_list]
    in_specs += [pl.BlockSpec(w.shape, lambda i: (0, 0)),
                 pl.BlockSpec((tm, n), lambda i: (i, 0)),
                 pl.BlockSpec((1, n), lambda i: (0, 0)),
                 pl.BlockSpec((1, n), lambda i: (0, 0))]
    return pl.pallas_call(
        functools.partial(_mm_ln_kernel, n_a=len(a_list)),
        out_shape=(jax.ShapeDtypeStruct((m, n), F32), jax.ShapeDtypeStruct((m, n), BF16)),
        grid=(m // tm,),
        in_specs=in_specs,
        out_specs=(pl.BlockSpec((tm, n), lambda i: (i, 0)),
                   pl.BlockSpec((tm, n), lambda i: (i, 0))),
        compiler_params=_params("parallel"),
        name="matmul_residual_ln",
    )(*a_list, w, res, gain.reshape(1, n), bias.reshape(1, n))


def _ple_ln_kernel(x_ref, xb_ref, ffn_ref, p_ref, wg_ref, wp_ref, g_ref, b_ref, o_ref, ob_ref):
    gate = jax.nn.sigmoid(jnp.dot(xb_ref[...], wg_ref[...], preferred_element_type=F32))
    proj = jnp.dot(p_ref[...].astype(BF16), wp_ref[...], preferred_element_type=F32)
    y = ALPHA * x_ref[...] + ffn_ref[...] + gate * proj
    out = _layer_norm_rows(y, g_ref[...], b_ref[...])
    o_ref[...] = out
    ob_ref[...] = out.astype(BF16)


def _ple_residual_ln(x, xb, ffn, p, wg, wp, gain, bias, tm=256):
    m, n = x.shape
    row = lambda i: (i, 0)
    const = lambda i: (0, 0)
    return pl.pallas_call(
        _ple_ln_kernel,
        out_shape=(jax.ShapeDtypeStruct((m, n), F32), jax.ShapeDtypeStruct((m, n), BF16)),
        grid=(m // tm,),
        in_specs=[pl.BlockSpec((tm, n), row), pl.BlockSpec((tm, n), row),
                  pl.BlockSpec((tm, n), row), pl.BlockSpec((tm, p.shape[1]), row),
                  pl.BlockSpec(wg.shape, const), pl.BlockSpec(wp.shape, const),
                  pl.BlockSpec((1, n), const), pl.BlockSpec((1, n), const)],
        out_specs=(pl.BlockSpec((tm, n), row), pl.BlockSpec((tm, n), row)),
        compiler_params=_params("parallel"),
        name="ple_residual_ln",
    )(x, xb, ffn, p, wg, wp, gain.reshape(1, n), bias.reshape(1, n))


def _conv_gate_kernel(gb_ref, gc_ref, xa_ref, w_ref, o_ref, buf, *, steps_per_seq):
    tm = gb_ref.shape[0]
    halo = SUBLANES

    @pl.when(pl.program_id(0) % steps_per_seq == 0)
    def _():
        buf[0:halo, :] = jnp.zeros((halo, buf.shape[1]), F32)

    u = gc_ref[...] * xa_ref[...]
    buf[halo:halo + tm, :] = u
    w = w_ref[...]
    conv = (w[0:1, :] * buf[halo - 2:halo - 2 + tm, :]
            + w[1:2, :] * buf[halo - 1:halo - 1 + tm, :]
            + w[2:3, :] * u)
    o_ref[...] = (gb_ref[...] * conv).astype(o_ref.dtype)
    buf[0:halo, :] = buf[tm:tm + halo, :]


def _conv_gate(h, conv_w, seq, tm=512):
    t = h.shape[0]
    ch = CONV_CH
    return pl.pallas_call(
        functools.partial(_conv_gate_kernel, steps_per_seq=seq // tm),
        out_shape=jax.ShapeDtypeStruct((t, ch), BF16),
        grid=(t // tm,),
        in_specs=[pl.BlockSpec((tm, ch), lambda i: (i, 0)),
                  pl.BlockSpec((tm, ch), lambda i: (i, 1)),
                  pl.BlockSpec((tm, ch), lambda i: (i, 2)),
                  pl.BlockSpec((CONV_WIDTH, ch), lambda i: (0, 0))],
        out_specs=pl.BlockSpec((tm, ch), lambda i: (i, 0)),
        scratch_shapes=[pltpu.VMEM((tm + SUBLANES, ch), F32)],
        compiler_params=_params("arbitrary"),
        name="conv_gate",
    )(h, h, h, conv_w)


def _rope(t, cosf, sinf):
    return t * cosf + pltpu.roll(t, ATT_HEAD_DIM // 2, 1) * sinf


def _attn_kernel(q_ref, kc_ref, kp_ref, vc_ref, vp_ref, cc_ref, sc_ref, cp_ref, sp_ref,
                 o_ref, q_s, k_s, v_s, acc_s, m_s, l_s):
    j = pl.program_id(2)
    bl = ATT_TOKENS
    blk = ATT_BLOCK
    cc = cc_ref[...]
    sc = sc_ref[...]
    q_s[...] = _rope(q_ref[...], cc, sc)
    k_s[bl:2 * bl, :] = _rope(kc_ref[...], cc, sc)
    k_s[0:bl, :] = _rope(kp_ref[...], cp_ref[...], sp_ref[...])
    v_s[bl:2 * bl, :] = vc_ref[...]
    v_s[0:bl, :] = vp_ref[...]

    qi = lax.broadcasted_iota(jnp.int32, (blk, 2 * blk), 0)
    kj = lax.broadcasted_iota(jnp.int32, (blk, 2 * blk), 1)
    dist = qi + blk - kj
    band = (dist >= 0) & (dist <= ATT_SPAN)
    band_first = band & (kj >= jnp.where(j > 0, 0, blk))
    scale = ATT_HEAD_DIM ** -0.5

    for branch, (_, d) in enumerate(DILATED_PAIRS):
        for g in range(bl // (blk * d)):
            for r in range(d):
                q_rows = pl.ds(g * blk * d + r, blk, stride=d) if d > 1 else pl.ds(g * blk, blk)
                k_start = bl + (g - 1) * blk * d + r
                k_rows = pl.ds(k_start, 2 * blk, stride=d) if d > 1 else pl.ds(k_start, 2 * blk)
                qs = q_s[q_rows, :].astype(BF16)
                ks = k_s[k_rows, :].astype(BF16)
                vs = v_s[k_rows, :].astype(BF16)
                s = lax.dot_general(qs, ks, _TRANS_B, preferred_element_type=F32) * scale
                s = jnp.where(band_first if g == 0 else band, s, NEG_INF)
                m_b = jnp.max(s, axis=-1, keepdims=True)
                e = jnp.exp(s - m_b)
                l_b = jnp.sum(e, axis=-1, keepdims=True)
                o_b = jnp.dot(e.astype(BF16), vs, preferred_element_type=F32)
                if branch == 0:
                    acc_s[q_rows, :] = o_b
                    m_s[q_rows, :] = jnp.broadcast_to(m_b, (blk, LANES))
                    l_s[q_rows, :] = jnp.broadcast_to(l_b, (blk, LANES))
                else:
                    m_old = m_s[q_rows, :]
                    m_new = jnp.maximum(m_old, m_b)
                    a_old = jnp.exp(m_old - m_new)
                    a_new = jnp.exp(m_b - m_new)
                    acc_s[q_rows, :] = a_old * acc_s[q_rows, :] + a_new * o_b
                    l_s[q_rows, :] = a_old * l_s[q_rows, :] + a_new * l_b
                    m_s[q_rows, :] = m_new

    o_ref[...] = (acc_s[...] / l_s[...]).astype(o_ref.dtype)


def _dilated_attention(h, cosf, sinf, batch, seq):
    t = h.shape[0]
    bl = ATT_TOKENS
    nj = seq // bl
    hd = ATT_HEAD_DIM
    q_col = 3 * CONV_CH // hd
    k_col = q_col + ATT_HEADS
    v_col = k_col + ATT_HEADS

    def cur(col):
        return lambda b, hh, j: (b * nj + j, col + hh)

    def prev(col):
        return lambda b, hh, j: (b * nj + jnp.maximum(j - 1, 0), col + hh)

    rows_cur = lambda b, hh, j: (b * nj + j, 0)
    rows_prev = lambda b, hh, j: (b * nj + jnp.maximum(j - 1, 0), 0)
    blk = (bl, hd)
    return pl.pallas_call(
        _attn_kernel,
        out_shape=jax.ShapeDtypeStruct((t, ATT_WIDTH), BF16),
        grid=(batch, ATT_HEADS, nj),
        in_specs=[pl.BlockSpec(blk, cur(q_col)),
                  pl.BlockSpec(blk, cur(k_col)), pl.BlockSpec(blk, prev(k_col)),
                  pl.BlockSpec(blk, cur(v_col)), pl.BlockSpec(blk, prev(v_col)),
                  pl.BlockSpec(blk, rows_cur), pl.BlockSpec(blk, rows_cur),
                  pl.BlockSpec(blk, rows_prev), pl.BlockSpec(blk, rows_prev)],
        out_specs=pl.BlockSpec(blk, lambda b, hh, j: (b * nj + j, hh)),
        scratch_shapes=[pltpu.VMEM((bl, hd), F32), pltpu.VMEM((2 * bl, hd), F32),
                        pltpu.VMEM((2 * bl, hd), F32), pltpu.VMEM((bl, hd), F32),
                        pltpu.VMEM((bl, LANES), F32), pltpu.VMEM((bl, LANES), F32)],
        compiler_params=_params("parallel", "parallel", "parallel"),
        name="dilated_attention",
    )(h, h, h, h, h, cosf, sinf, cosf, sinf)


def _log_sigmoid(x):
    return jnp.minimum(x, 0.0) - jnp.log(1.0 + jnp.exp(-jnp.abs(x)))


def _mlstm_kernel(q_ref, k_ref, v_ref, og_ref, g_ref, bias_ref, o_ref, ct_s, n_s, m_s):
    nh, dk, dv, cl = MLSTM_HEADS, MLSTM_QK_DIM, MLSTM_V_DIM, MLSTM_CHUNK
    n_seq = q_ref.shape[0]

    @pl.when(pl.program_id(0) == 0)
    def _():
        ct_s[...] = jnp.zeros(ct_s.shape, F32)
        n_s[...] = jnp.zeros(n_s.shape, F32)
        m_s[...] = jnp.zeros(m_s.shape, F32)

    row = lax.broadcasted_iota(jnp.int32, (cl, cl), 0)
    col = lax.broadcasted_iota(jnp.int32, (cl, cl), 1)
    causal = col <= row
    ones = jnp.ones((cl, LANES), BF16)

    for b in range(n_seq):
        gates = g_ref[b] + bias_ref[...]
        gates_t = gates.T
        for hd in range(nh):
            st = b * nh + hd
            ig_col = gates[:, hd:hd + 1]
            ig_row = gates_t[hd:hd + 1, :]
            lf_col = _log_sigmoid(gates[:, nh + hd:nh + hd + 1])
            lf_row = _log_sigmoid(gates_t[nh + hd:nh + hd + 1, :])
            bcum_col = jnp.sum(jnp.where(causal, lf_row, 0.0), axis=1, keepdims=True)
            bcum_row = jnp.sum(jnp.where(row <= col, lf_col, 0.0), axis=0, keepdims=True)
            btot = bcum_col[cl - 1:cl, :]

            qb = q_ref[b, :, hd * dk:(hd + 1) * dk].astype(BF16)
            kf = k_ref[b, :, hd * dk:(hd + 1) * dk] * (dk ** -0.5)
            vf = v_ref[b, :, hd * dv:(hd + 1) * dv]
            kb = kf.astype(BF16)
            ct_prev = ct_s[st]
            n_prev = n_s[st]
            m_prev = m_s[st][:, 0:1]

            dmat = jnp.where(causal, bcum_col - bcum_row + ig_row, NEG_INF)
            g_inter = bcum_col + m_prev
            m_t = jnp.maximum(g_inter, jnp.max(dmat, axis=1, keepdims=True))
            qk = lax.dot_general(qb, kb, _TRANS_B, preferred_element_type=F32)
            sc = (qk * jnp.exp(dmat - m_t)).astype(BF16)
            a_inter = jnp.exp(g_inter - m_t)
            num = (a_inter * jnp.dot(qb, ct_prev.astype(BF16), preferred_element_type=F32)
                   + jnp.dot(sc, vf.astype(BF16), preferred_element_type=F32))
            n_rows = jnp.broadcast_to(n_prev, (LANES, dk)).astype(BF16)
            den = (a_inter * lax.dot_general(qb, n_rows, _TRANS_B, preferred_element_type=F32)
                   + jnp.dot(sc, ones, preferred_element_type=F32))
            scale = 1.0 / jnp.maximum(jnp.abs(den), jnp.exp(-m_t))
            hh = num * jnp.tile(scale, (1, dv // LANES))
            o_ref[b, :, hd * dv:(hd + 1) * dv] = (
                jax.nn.sigmoid(og_ref[b, :, hd * dv:(hd + 1) * dv]) * hh).astype(o_ref.dtype)

            w_loc = btot - bcum_col + ig_col
            m_loc = jnp.max(w_loc, axis=0, keepdims=True)
            a_loc = jnp.exp(w_loc - m_loc)
            ct_loc = jnp.dot(kf.T.astype(BF16), (vf * a_loc).astype(BF16),
                             preferred_element_type=F32)
            n_loc = jnp.sum(a_loc * kf, axis=0, keepdims=True)
            m_new = jnp.maximum(btot + m_prev, m_loc)
            a_old = jnp.exp(btot + m_prev - m_new)
            a_new = jnp.exp(m_loc - m_new)
            ct_s[st] = a_old * ct_prev + a_new * ct_loc
            n_s[st] = a_old * n_prev + a_new * n_loc
            m_s[st] = jnp.broadcast_to(m_new, (1, LANES))


def _mlstm(h, gates, gate_bias_row, batch, seq):
    t = h.shape[0]
    nh, dk, dv, cl = MLSTM_HEADS, MLSTM_QK_DIM, MLSTM_V_DIM, MLSTM_CHUNK
    h3 = h.reshape(batch, seq, h.shape[1])
    g3 = gates.reshape(batch, seq, LANES)
    cols = lambda col: (lambda c: (0, c, col))
    out = pl.pallas_call(
        _mlstm_kernel,
        out_shape=jax.ShapeDtypeStruct((batch, seq, nh * dv), BF16),
        grid=(seq // cl,),
        in_specs=[pl.BlockSpec((batch, cl, nh * dk), cols(0)),
                  pl.BlockSpec((batch, cl, nh * dk), cols(1)),
                  pl.BlockSpec((batch, cl, nh * dv), cols(1)),
                  pl.BlockSpec((batch, cl, nh * dv), cols(2)),
                  pl.BlockSpec((batch, cl, LANES), cols(0)),
                  pl.BlockSpec((1, LANES), lambda c: (0, 0))],
        out_specs=pl.BlockSpec((batch, cl, nh * dv), cols(0)),
        scratch_shapes=[pltpu.VMEM((batch * nh, dk, dv), F32),
                        pltpu.VMEM((batch * nh, 1, dk), F32),
                        pltpu.VMEM((batch * nh, 1, LANES), F32)],
        compiler_params=_params("arbitrary"),
        name="mlstm",
    )(h3, h3, h3, h3, g3, gate_bias_row)
    return out.reshape(t, nh * dv)


_E1_GROUPS = PEER_NKEYS // SUBLANES
_E1_ROWS_PER_PASS = 2
_BF16_PER_WORD = 2


def _pack_bf16(x):
    return pltpu.bitcast(x.astype(BF16), jnp.uint32)


def _unpack_bf16(words):
    return pltpu.bitcast(words, BF16)


def _bf16_rows(row, n_rows):
    tile_rows = SUBLANES * _BF16_PER_WORD
    return jnp.tile(jnp.broadcast_to(row, (tile_rows, row.shape[1])).astype(BF16),
                    (n_rows // tile_rows, 1))


def _sorting_network(n):
    pairs = []
    p = 1
    while p < n:
        k = p
        while k >= 1:
            for j in range(k % p, n - k, 2 * k):
                for i in range(min(k, n - j - k)):
                    if (i + j) // (2 * p) == (i + j + k) // (2 * p):
                        pairs.append((i + j, i + j + k))
            k //= 2
        p *= 2
    return pairs


def _pop_sorted_columns(levels, k, extra=None, on_pop=None):
    levels = list(levels)
    depth = len(levels)
    vals = []
    for t in range(k):
        top = levels[0] if extra is None else jnp.maximum(levels[0], extra)
        mx = jnp.max(top, axis=0, keepdims=True)
        vals.append(mx)
        if on_pop is not None:
            on_pop(mx)
        hit = levels[0] == mx
        if extra is not None:
            extra = jnp.where(extra == mx, NEG_INF, extra)
        for lv in range(min(depth, k - t)):
            below = levels[lv + 1] if lv + 1 < depth else NEG_INF
            levels[lv] = jnp.where(hit, below, levels[lv])
    return vals


def _count_passing(rows, passes):
    def probe(bits, index, step):
        if not bits:
            return rows[index + step - 1]
        (mask, weight), rest = bits[0], bits[1:]
        return jnp.where(mask, probe(rest, index + weight, step), probe(rest, index, step))

    bits = []
    count = jnp.where(passes(rows[-1]), 1.0, 0.0)
    step = len(rows) // 2
    while step >= 1:
        mask = passes(probe(bits, 0, step))
        bits.append((mask, step))
        count = count + jnp.where(mask, float(step), 0.0)
        step //= 2
    return count


def _top_values(s, k):
    levels = [s[i * SUBLANES:(i + 1) * SUBLANES] for i in range(s.shape[0] // SUBLANES)]
    for i, j in _sorting_network(len(levels)):
        levels[i], levels[j] = jnp.maximum(levels[i], levels[j]), jnp.minimum(levels[i], levels[j])
    return _pop_sorted_columns(levels, k)


def _peer_topk_kernel(q_ref, keys_ref, r2_ref, p2_ref, n_ref, p1_ref, s1_s, s2_s):
    kk = PEER_TOPK
    assert kk == 2 * SUBLANES
    half = PEER_QDIM // 2
    tb = q_ref.shape[0]
    q = q_ref[...]
    s1_s[...] = lax.dot_general(keys_ref[0, 0], q[:, :half], _TRANS_B, preferred_element_type=F32)
    s2_s[...] = lax.dot_general(keys_ref[0, 1], q[:, half:], _TRANS_B, preferred_element_type=F32)

    def chunk(c):
        lanes = pl.ds(pl.multiple_of(c * LANES, LANES), LANES)
        s1 = s1_s[:, lanes]
        s2 = s2_s[:, lanes]
        v1 = _top_values(s1, kk)
        v2 = _top_values(s2, kk)
        v2_lo = jnp.concatenate(v2[:SUBLANES], axis=0)
        v2_hi = jnp.concatenate(v2[SUBLANES:], axis=0)
        m = v1[0] + v2[0]
        exps = []
        sums = _pop_sorted_columns([v1[a] + v2_lo for a in range(kk)], kk, extra=v1[0] + v2_hi,
                                   on_pop=lambda mx: exps.append(jnp.exp(mx - m)))
        thr = sums[-1]
        den = functools.reduce(lambda a, b: a + b, exps)
        p2_ref[0, :, lanes] = _pack_bf16(jnp.exp(s2 - v2[0]) / den)
        rank2 = _count_passing(v2, lambda t: t > s2)
        n = _count_passing(v2, lambda t: s1 + t >= thr)
        r2_ref[0, :, lanes] = _pack_bf16(rank2)
        p1 = jnp.exp(s1 - v1[0])
        for g in range(_E1_GROUPS):
            rows = slice(g * SUBLANES, (g + 1) * SUBLANES)
            n_ref[g, 0, :, lanes] = n[rows]
            p1_ref[g, 0, :, lanes] = p1[rows]

    def chunk_pair(i, carry):
        chunk(2 * i)
        chunk(2 * i + 1)
        return carry

    lax.fori_loop(0, tb // (2 * LANES), chunk_pair, 0)


def _peer_topk(qb, keys_b, tb=512):
    t = qb.shape[0]
    nh, nk = PEER_HEADS, PEER_NKEYS
    e2_spec = pl.BlockSpec((1, nk // _BF16_PER_WORD, tb), lambda i, h: (h, 0, i))
    e1_spec = pl.BlockSpec((_E1_GROUPS, 1, SUBLANES, tb), lambda i, h: (0, h, 0, i))
    return pl.pallas_call(
        _peer_topk_kernel,
        out_shape=(jax.ShapeDtypeStruct((nh, nk // _BF16_PER_WORD, t), jnp.uint32),
                   jax.ShapeDtypeStruct((nh, nk // _BF16_PER_WORD, t), jnp.uint32),
                   jax.ShapeDtypeStruct((_E1_GROUPS, nh, SUBLANES, t), F32),
                   jax.ShapeDtypeStruct((_E1_GROUPS, nh, SUBLANES, t), F32)),
        grid=(t // tb, nh),
        in_specs=[pl.BlockSpec((tb, PEER_QDIM), lambda i, h: (i, h)),
                  pl.BlockSpec((1, 2, nk, PEER_QDIM // 2), lambda i, h: (h, 0, 0, 0))],
        out_specs=(e2_spec, e2_spec, e1_spec, e1_spec),
        scratch_shapes=[pltpu.VMEM((nk, tb), F32), pltpu.VMEM((nk, tb), F32)],
        compiler_params=_params("parallel", "parallel"),
        name="peer_topk",
    )(qb, keys_b)


def _gelu_exact(x):
    return 0.5 * x * (1.0 + lax.erf(x * (2.0 ** -0.5)))


def _zero_after(v):
    bits = pltpu.bitcast(v, jnp.uint32)
    return pltpu.bitcast((bits >> 16) >> 16, F32)[0:1, :]


def _peer_tile_activations(hid_s, act_s, group, row0, r2_ref, p2_ref, n_ref, p1_ref, after):
    nh, nk = PEER_HEADS, PEER_NKEYS
    te, tb = hid_s.shape
    zero = jnp.zeros((nk, LANES), BF16)
    for c in range(tb // LANES):
        lanes = slice(c * LANES, (c + 1) * LANES)
        for le1_0 in range(0, te // nk, _E1_ROWS_PER_PASS):
            le1s = range(le1_0, le1_0 + _E1_ROWS_PER_PASS)
            w = [zero for _ in le1s]
            for h in range(nh):
                rank2 = _unpack_bf16(r2_ref[h, :, lanes])
                p2 = _unpack_bf16(p2_ref[h, :, lanes])
                for j, le1 in enumerate(le1s):
                    r = row0 + le1
                    n_rows = _bf16_rows(n_ref[group, h, r:r + 1, lanes] + after, nk)
                    p1_rows = _bf16_rows(p1_ref[group, h, r:r + 1, lanes], nk)
                    w[j] = w[j] + jnp.where(rank2 < n_rows, p2 * p1_rows, zero)
            for j, le1 in enumerate(le1s):
                rows = slice(le1 * nk, (le1 + 1) * nk)
                act = w[j] * _gelu_exact(hid_s[rows, lanes]).astype(BF16)
                act_s[rows, lanes] = act
            after = _zero_after(act[0:2 * SUBLANES, :])
    return after


def _peer_dense_kernel(x_ref, u_ref, vt_ref, r2_ref, p2_ref, n_ref, p1_ref, o_ref,
                       acc_s, hid0_s, hid1_s, act0_s, act1_s):
    te = hid0_s.shape[0]
    rows_per_tile = te // PEER_NKEYS
    s = pl.program_id(1)
    last = pl.num_programs(1) - 1
    no_dep = jnp.zeros((1, LANES), F32)

    def hidden(hid_s, half, x):
        hid_s[...] = lax.dot_general(u_ref[half * te:(half + 1) * te, :], x, _TRANS_B,
                                     preferred_element_type=F32)

    def accumulate(act_s, half):
        acc_s[...] += jnp.dot(vt_ref[:, half * te:(half + 1) * te], act_s[...],
                              preferred_element_type=F32)

    def activations(hid_s, act_s, group, row0, after):
        return _peer_tile_activations(hid_s, act_s, group, row0, r2_ref, p2_ref, n_ref, p1_ref,
                                      after)

    @pl.when(s == 0)
    def _():
        acc_s[...] = jnp.zeros(acc_s.shape, F32)
        x = x_ref[...]
        hidden(hid0_s, 0, x)
        hidden(hid1_s, 1, x)
        activations(hid0_s, act0_s, 0, 0, no_dep)

    @pl.when((s > 0) & (s < last))
    def _():
        x = x_ref[...]
        after = activations(hid1_s, act1_s, s - 1, rows_per_tile, no_dep)
        accumulate(act0_s, 0)
        hidden(hid0_s, 0, x)
        hidden(hid1_s, 1, x)
        activations(hid0_s, act0_s, s, 0, after)
        accumulate(act1_s, 1)

    @pl.when(s == last)
    def _():
        activations(hid1_s, act1_s, s - 1, rows_per_tile, no_dep)
        accumulate(act0_s, 0)
        accumulate(act1_s, 1)
        o_ref[...] = acc_s[...].T


def _peer_dense(xb, u_b, v_b, r2, p2, n_rows, p1_rows, tb=512):
    t, dm = xb.shape
    nh, nk = PEER_HEADS, PEER_NKEYS
    ne = u_b.shape[0]
    te = SUBLANES * nk // 2
    n_pairs = ne // (2 * te)
    assert n_pairs == _E1_GROUPS
    vt_b = v_b.reshape(n_pairs, 2 * te, dm).transpose(0, 2, 1)
    e2_spec = pl.BlockSpec((nh, nk // _BF16_PER_WORD, tb), lambda i, s: (0, 0, i))
    e1_spec = pl.BlockSpec((_E1_GROUPS, nh, SUBLANES, tb), lambda i, s: (0, 0, 0, i))
    return pl.pallas_call(
        _peer_dense_kernel,
        out_shape=jax.ShapeDtypeStruct((t, dm), F32),
        grid=(t // tb, n_pairs + 1),
        in_specs=[pl.BlockSpec((tb, dm), lambda i, s: (i, 0)),
                  pl.BlockSpec((2 * te, dm), lambda i, s: (jnp.minimum(s, n_pairs - 1), 0)),
                  pl.BlockSpec((None, dm, 2 * te), lambda i, s: (jnp.maximum(s - 1, 0), 0, 0)),
                  e2_spec, e2_spec, e1_spec, e1_spec],
        out_specs=pl.BlockSpec((tb, dm), lambda i, s: (i, 0)),
        scratch_shapes=[pltpu.VMEM((dm, tb), F32),
                        pltpu.VMEM((te, tb), F32), pltpu.VMEM((te, tb), F32),
                        pltpu.VMEM((te, tb), BF16), pltpu.VMEM((te, tb), BF16)],
        compiler_params=_params("parallel", "arbitrary"),
        name="peer_dense",
    )(xb, u_b, vt_b, r2, p2, n_rows, p1_rows)


def _peer(xb, w_q_b, keys_b, u_b, v_b):
    qb = _matmul(xb, w_q_b, BF16)
    r2, p2, n_rows, p1_rows = _peer_topk(qb, keys_b)
    return _peer_dense(xb, u_b, v_b, r2, p2, n_rows, p1_rows)


def _rope_tables(positions):
    half = ATT_HEAD_DIM // 2
    inv_freq = jnp.power(ROPE_THETA, -2.0 * jnp.arange(half, dtype=F32) / ATT_HEAD_DIM)
    ang = positions.astype(F32).reshape(-1, 1) * inv_freq
    cos, sin = jnp.cos(ang), jnp.sin(ang)
    return jnp.concatenate([cos, cos], axis=-1), jnp.concatenate([-sin, sin], axis=-1)


def kernel(x, p, positions, even_w_in, even_conv_w, even_w_out, odd_w_in, odd_gate_bias,
           odd_w_out, peer_w_q, peer_sub_keys, peer_u, peer_v, ple_w_proj, ple_w_gate,
           ln_gain, ln_bias):
    batch, seq, dm = x.shape
    t = batch * seq
    assert dm == D_MODEL and seq % ATT_TOKENS == 0 and seq % MLSTM_CHUNK == 0
    xf = x.reshape(t, dm)
    xb = xf.astype(BF16)
    cosf, sinf = _rope_tables(positions)

    for layer in range(DEPTH):
        j = layer // 2
        if layer % 2 == 0:
            h = _matmul(xb, even_w_in[j].astype(BF16), F32)
            y_conv = _conv_gate(h, even_conv_w[j], seq)
            y_att = _dilated_attention(h, cosf, sinf, batch, seq)
            mixed = [y_conv, y_att]
            w_out = even_w_out[j]
        else:
            w_in = odd_w_in[j]
            h = _matmul(xb, w_in[:, :ODD_MAIN].astype(BF16), F32)
            n_gates = 2 * MLSTM_HEADS
            w_gates = jnp.pad(w_in[:, ODD_MAIN:], ((0, 0), (0, LANES - n_gates))).astype(BF16)
            gates = _matmul(xb, w_gates, F32)
            bias_row = jnp.pad(odd_gate_bias[j].reshape(1, n_gates), ((0, 0), (0, LANES - n_gates)))
            mixed = [_mlstm(h, gates, bias_row, batch, seq)]
            w_out = odd_w_out[j]
        xf, xb = _matmul_residual_ln(mixed, w_out.astype(BF16), xf,
                                     ln_gain[layer, 0], ln_bias[layer, 0])
        ffn = _peer(xb, peer_w_q[layer].astype(BF16), peer_sub_keys[layer].astype(BF16),
                    peer_u[layer].astype(BF16), peer_v[layer].astype(BF16))
        xf, xb = _ple_residual_ln(xf, xb, ffn, p[layer].reshape(t, PLE_DIM),
                                  ple_w_gate[layer].astype(BF16), ple_w_proj[layer].astype(BF16),
                                  ln_gain[layer, 1], ln_bias[layer, 1])
    return xf.reshape(batch, seq, dm)
```

```python
import functools

import jax
import jax.numpy as jnp
from jax import lax
from jax.experimental import pallas as pl
from jax.experimental.pallas import tpu as pltpu

D_MODEL = 2048
DEPTH = 4
CONV_CH = D_MODEL // 2
CONV_WIDTH = 3
ATT_HEADS = 8
ATT_HEAD_DIM = 128
ATT_WIDTH = ATT_HEADS * ATT_HEAD_DIM
DILATED_PAIRS = ((128, 1), (512, 4), (2048, 16))
ROPE_THETA = 10000.0
MLSTM_HEADS = 8
MLSTM_QK_DIM = 128
MLSTM_V_DIM = D_MODEL // MLSTM_HEADS
MLSTM_CHUNK = 128
ODD_MAIN = 2 * MLSTM_HEADS * MLSTM_QK_DIM + 2 * MLSTM_HEADS * MLSTM_V_DIM
PEER_HEADS = 8
PEER_NKEYS = 128
PEER_QDIM = 256
PEER_TOPK = 16
PLE_DIM = 256
ALPHA = (2.0 * DEPTH) ** 0.25
LN_EPS = 1e-5

LANES = 128
SUBLANES = 8
MXU_TILE = 256
VMEM_LIMIT_BYTES = 56 * 1024 * 1024

ATT_BLOCK = 128
ATT_SPAN = 128
ATT_TOKENS = ATT_BLOCK * max(d for _, d in DILATED_PAIRS)

NEG_INF = float("-inf")
F32 = jnp.float32
BF16 = jnp.bfloat16

_TRANS_B = (((1,), (1,)), ((), ()))


def _params(*semantics):
    return pltpu.CompilerParams(dimension_semantics=semantics,
                                vmem_limit_bytes=VMEM_LIMIT_BYTES)


def _layer_norm_rows(y, gain, bias):
    mu = jnp.mean(y, axis=-1, keepdims=True)
    yc = y - mu
    var = jnp.mean(yc * yc, axis=-1, keepdims=True)
    return yc * lax.rsqrt(var + LN_EPS) * gain + bias


def _mm_kernel(a_ref, w_ref, o_ref):
    o_ref[...] = jnp.dot(a_ref[...], w_ref[...],
                         preferred_element_type=F32).astype(o_ref.dtype)


def _matmul(a, w, out_dtype, tm=512, tn=None):
    m, k = a.shape
    n = w.shape[1]
    tn = min(n, 2048) if tn is None else tn
    return pl.pallas_call(
        _mm_kernel,
        out_shape=jax.ShapeDtypeStruct((m, n), out_dtype),
        grid=(n // tn, m // tm),
        in_specs=[pl.BlockSpec((tm, k), lambda j, i: (i, 0)),
                  pl.BlockSpec((k, tn), lambda j, i: (0, j))],
        out_specs=pl.BlockSpec((tm, tn), lambda j, i: (i, j)),
        compiler_params=_params("parallel", "parallel"),
        name="matmul",
    )(a, w)


def _mm_ln_kernel(*refs, n_a):
    a_refs = refs[:n_a]
    w_ref, res_ref, g_ref, b_ref, o_ref, ob_ref = refs[n_a:]
    for r0 in range(0, res_ref.shape[0], MXU_TILE):
        rows = slice(r0, r0 + MXU_TILE)
        acc = None
        off = 0
        for a_ref in a_refs:
            k = a_ref.shape[1]
            part = jnp.dot(a_ref[rows, :], w_ref[off:off + k, :], preferred_element_type=F32)
            acc = part if acc is None else acc + part
            off += k
        y = ALPHA * res_ref[rows, :] + acc
        out = _layer_norm_rows(y, g_ref[...], b_ref[...])
        o_ref[rows, :] = out
        ob_ref[rows, :] = out.astype(BF16)


def _matmul_residual_ln(a_list, w, res, gain, bias, tm=512):
    m = res.shape[0]
    n = w.shape[1]
    in_specs = [pl.BlockSpec((tm, a.shape[1]), lambda i: (i, 0)) for a in a_list]
    in_specs += [pl.BlockSpec(w.shape, lambda i: (0, 0)),
                 pl.BlockSpec((tm, n), lambda i: (i, 0)),
                 pl.BlockSpec((1, n), lambda i: (0, 0)),
                 pl.BlockSpec((1, n), lambda i: (0, 0))]
    return pl.pallas_call(
        functools.partial(_mm_ln_kernel, n_a=len(a_list)),
        out_shape=(jax.ShapeDtypeStruct((m, n), F32), jax.ShapeDtypeStruct((m, n), BF16)),
        grid=(m // tm,),
        in_specs=in_specs,
        out_specs=(pl.BlockSpec((tm, n), lambda i: (i, 0)),
                   pl.BlockSpec((tm, n), lambda i: (i, 0))),
        compiler_params=_params("parallel"),
        name="matmul_residual_ln",
    )(*a_list, w, res, gain.reshape(1, n), bias.reshape(1, n))


def _ple_ln_kernel(x_ref, xb_ref, ffn_ref, p_ref, wg_ref, wp_ref, g_ref, b_ref, o_ref, ob_ref):
    gate = jax.nn.sigmoid(jnp.dot(xb_ref[...], wg_ref[...], preferred_element_type=F32))
    proj = jnp.dot(p_ref[...].astype(BF16), wp_ref[...], preferred_element_type=F32)
    y = ALPHA * x_ref[...] + ffn_ref[...] + gate * proj
    out = _layer_norm_rows(y, g_ref[...], b_ref[...])
    o_ref[...] = out
    ob_ref[...] = out.astype(BF16)


def _ple_residual_ln(x, xb, ffn, p, wg, wp, gain, bias, tm=256):
    m, n = x.shape
    row = lambda i: (i, 0)
    const = lambda i: (0, 0)
    return pl.pallas_call(
        _ple_ln_kernel,
        out_shape=(jax.ShapeDtypeStruct((m, n), F32), jax.ShapeDtypeStruct((m, n), BF16)),
        grid=(m // tm,),
        in_specs=[pl.BlockSpec((tm, n), row), pl.BlockSpec((tm, n), row),
                  pl.BlockSpec((tm, n), row), pl.BlockSpec((tm, p.shape[1]), row),
                  pl.BlockSpec(wg.shape, const), pl.BlockSpec(wp.shape, const),
                  pl.BlockSpec((1, n), const), pl.BlockSpec((1, n), const)],
        out_specs=(pl.BlockSpec((tm, n), row), pl.BlockSpec((tm, n), row)),
        compiler_params=_params("parallel"),
        name="ple_residual_ln",
    )(x, xb, ffn, p, wg, wp, gain.reshape(1, n), bias.reshape(1, n))


def _conv_gate_kernel(gb_ref, gc_ref, xa_ref, w_ref, o_ref, buf, *, steps_per_seq):
    tm = gb_ref.shape[0]
    halo = SUBLANES

    @pl.when(pl.program_id(0) % steps_per_seq == 0)
    def _():
        buf[0:halo, :] = jnp.zeros((halo, buf.shape[1]), F32)

    u = gc_ref[...] * xa_ref[...]
    buf[halo:halo + tm, :] = u
    w = w_ref[...]
    conv = (w[0:1, :] * buf[halo - 2:halo - 2 + tm, :]
            + w[1:2, :] * buf[halo - 1:halo - 1 + tm, :]
            + w[2:3, :] * u)
    o_ref[...] = (gb_ref[...] * conv).astype(o_ref.dtype)
    buf[0:halo, :] = buf[tm:tm + halo, :]


def _conv_gate(h, conv_w, seq, tm=512):
    t = h.shape[0]
    ch = CONV_CH
    return pl.pallas_call(
        functools.partial(_conv_gate_kernel, steps_per_seq=seq // tm),
        out_shape=jax.ShapeDtypeStruct((t, ch), BF16),
        grid=(t // tm,),
        in_specs=[pl.BlockSpec((tm, ch), lambda i: (i, 0)),
                  pl.BlockSpec((tm, ch), lambda i: (i, 1)),
                  pl.BlockSpec((tm, ch), lambda i: (i, 2)),
                  pl.BlockSpec((CONV_WIDTH, ch), lambda i: (0, 0))],
        out_specs=pl.BlockSpec((tm, ch), lambda i: (i, 0)),
        scratch_shapes=[pltpu.VMEM((tm + SUBLANES, ch), F32)],
        compiler_params=_params("arbitrary"),
        name="conv_gate",
    )(h, h, h, conv_w)


def _rope(t, cosf, sinf):
    return t * cosf + pltpu.roll(t, ATT_HEAD_DIM // 2, 1) * sinf


def _attn_kernel(q_ref, kc_ref, kp_ref, vc_ref, vp_ref, cc_ref, sc_ref, cp_ref, sp_ref,
                 o_ref, q_s, k_s, v_s, acc_s, m_s, l_s):
    j = pl.program_id(2)
    bl = ATT_TOKENS
    blk = ATT_BLOCK
    cc = cc_ref[...]
    sc = sc_ref[...]
    q_s[...] = _rope(q_ref[...], cc, sc)
    k_s[bl:2 * bl, :] = _rope(kc_ref[...], cc, sc)
    k_s[0:bl, :] = _rope(kp_ref[...], cp_ref[...], sp_ref[...])
    v_s[bl:2 * bl, :] = vc_ref[...]
    v_s[0:bl, :] = vp_ref[...]

    qi = lax.broadcasted_iota(jnp.int32, (blk, 2 * blk), 0)
    kj = lax.broadcasted_iota(jnp.int32, (blk, 2 * blk), 1)
    dist = qi + blk - kj
    band = (dist >= 0) & (dist <= ATT_SPAN)
    band_first = band & (kj >= jnp.where(j > 0, 0, blk))
    scale = ATT_HEAD_DIM ** -0.5

    for branch, (_, d) in enumerate(DILATED_PAIRS):
        for g in range(bl // (blk * d)):
            for r in range(d):
                q_rows = pl.ds(g * blk * d + r, blk, stride=d) if d > 1 else pl.ds(g * blk, blk)
                k_start = bl + (g - 1) * blk * d + r
                k_rows = pl.ds(k_start, 2 * blk, stride=d) if d > 1 else pl.ds(k_start, 2 * blk)
                qs = q_s[q_rows, :].astype(BF16)
                ks = k_s[k_rows, :].astype(BF16)
                vs = v_s[k_rows, :].astype(BF16)
                s = lax.dot_general(qs, ks, _TRANS_B, preferred_element_type=F32) * scale
                s = jnp.where(band_first if g == 0 else band, s, NEG_INF)
                m_b = jnp.max(s, axis=-1, keepdims=True)
                e = jnp.exp(s - m_b)
                l_b = jnp.sum(e, axis=-1, keepdims=True)
                o_b = jnp.dot(e.astype(BF16), vs, preferred_element_type=F32)
                if branch == 0:
                    acc_s[q_rows, :] = o_b
                    m_s[q_rows, :] = jnp.broadcast_to(m_b, (blk, LANES))
                    l_s[q_rows, :] = jnp.broadcast_to(l_b, (blk, LANES))
                else:
                    m_old = m_s[q_rows, :]
                    m_new = jnp.maximum(m_old, m_b)
                    a_old = jnp.exp(m_old - m_new)
                    a_new = jnp.exp(m_b - m_new)
                    acc_s[q_rows, :] = a_old * acc_s[q_rows, :] + a_new * o_b
                    l_s[q_rows, :] = a_old * l_s[q_rows, :] + a_new * l_b
                    m_s[q_rows, :] = m_new

    o_ref[...] = (acc_s[...] / l_s[...]).astype(o_ref.dtype)


def _dilated_attention(h, cosf, sinf, batch, seq):
    t = h.shape[0]
    bl = ATT_TOKENS
    nj = seq // bl
    hd = ATT_HEAD_DIM
    q_col = 3 * CONV_CH // hd
    k_col = q_col + ATT_HEADS
    v_col = k_col + ATT_HEADS

    def cur(col):
        return lambda b, hh, j: (b * nj + j, col + hh)

    def prev(col):
        return lambda b, hh, j: (b * nj + jnp.maximum(j - 1, 0), col + hh)

    rows_cur = lambda b, hh, j: (b * nj + j, 0)
    rows_prev = lambda b, hh, j: (b * nj + jnp.maximum(j - 1, 0), 0)
    blk = (bl, hd)
    return pl.pallas_call(
        _attn_kernel,
        out_shape=jax.ShapeDtypeStruct((t, ATT_WIDTH), BF16),
        grid=(batch, ATT_HEADS, nj),
        in_specs=[pl.BlockSpec(blk, cur(q_col)),
                  pl.BlockSpec(blk, cur(k_col)), pl.BlockSpec(blk, prev(k_col)),
                  pl.BlockSpec(blk, cur(v_col)), pl.BlockSpec(blk, prev(v_col)),
                  pl.BlockSpec(blk, rows_cur), pl.BlockSpec(blk, rows_cur),
                  pl.BlockSpec(blk, rows_prev), pl.BlockSpec(blk, rows_prev)],
        out_specs=pl.BlockSpec(blk, lambda b, hh, j: (b * nj + j, hh)),
        scratch_shapes=[pltpu.VMEM((bl, hd), F32), pltpu.VMEM((2 * bl, hd), F32),
                        pltpu.VMEM((2 * bl, hd), F32), pltpu.VMEM((bl, hd), F32),
                        pltpu.VMEM((bl, LANES), F32), pltpu.VMEM((bl, LANES), F32)],
        compiler_params=_params("parallel", "parallel", "parallel"),
        name="dilated_attention",
    )(h, h, h, h, h, cosf, sinf, cosf, sinf)


def _log_sigmoid(x):
    return jnp.minimum(x, 0.0) - jnp.log(1.0 + jnp.exp(-jnp.abs(x)))


def _mlstm_kernel(q_ref, k_ref, v_ref, og_ref, g_ref, bias_ref, o_ref, ct_s, n_s, m_s):
    nh, dk, dv, cl = MLSTM_HEADS, MLSTM_QK_DIM, MLSTM_V_DIM, MLSTM_CHUNK
    n_seq = q_ref.shape[0]

    @pl.when(pl.program_id(0) == 0)
    def _():
        ct_s[...] = jnp.zeros(ct_s.shape, F32)
        n_s[...] = jnp.zeros(n_s.shape, F32)
        m_s[...] = jnp.zeros(m_s.shape, F32)

    row = lax.broadcasted_iota(jnp.int32, (cl, cl), 0)
    col = lax.broadcasted_iota(jnp.int32, (cl, cl), 1)
    causal = col <= row
    ones = jnp.ones((cl, LANES), BF16)

    for b in range(n_seq):
        gates = g_ref[b] + bias_ref[...]
        gates_t = gates.T
        for hd in range(nh):
            st = b * nh + hd
            ig_col = gates[:, hd:hd + 1]
            ig_row = gates_t[hd:hd + 1, :]
            lf_col = _log_sigmoid(gates[:, nh + hd:nh + hd + 1])
            lf_row = _log_sigmoid(gates_t[nh + hd:nh + hd + 1, :])
            bcum_col = jnp.sum(jnp.where(causal, lf_row, 0.0), axis=1, keepdims=True)
            bcum_row = jnp.sum(jnp.where(row <= col, lf_col, 0.0), axis=0, keepdims=True)
            btot = bcum_col[cl - 1:cl, :]

            qb = q_ref[b, :, hd * dk:(hd + 1) * dk].astype(BF16)
            kf = k_ref[b, :, hd * dk:(hd + 1) * dk] * (dk ** -0.5)
            vf = v_ref[b, :, hd * dv:(hd + 1) * dv]
            kb = kf.astype(BF16)
            ct_prev = ct_s[st]
            n_prev = n_s[st]
            m_prev = m_s[st][:, 0:1]

            dmat = jnp.where(causal, bcum_col - bcum_row + ig_row, NEG_INF)
            g_inter = bcum_col + m_prev
            m_t = jnp.maximum(g_inter, jnp.max(dmat, axis=1, keepdims=True))
            qk = lax.dot_general(qb, kb, _TRANS_B, preferred_element_type=F32)
            sc = (qk * jnp.exp(dmat - m_t)).astype(BF16)
            a_inter = jnp.exp(g_inter - m_t)
            num = (a_inter * jnp.dot(qb, ct_prev.astype(BF16), preferred_element_type=F32)
                   + jnp.dot(sc, vf.astype(BF16), preferred_element_type=F32))
            n_rows = jnp.broadcast_to(n_prev, (LANES, dk)).astype(BF16)
            den = (a_inter * lax.dot_general(qb, n_rows, _TRANS_B, preferred_element_type=F32)
                   + jnp.dot(sc, ones, preferred_element_type=F32))
            scale = 1.0 / jnp.maximum(jnp.abs(den), jnp.exp(-m_t))
            hh = num * jnp.tile(scale, (1, dv // LANES))
            o_ref[b, :, hd * dv:(hd + 1) * dv] = (
                jax.nn.sigmoid(og_ref[b, :, hd * dv:(hd + 1) * dv]) * hh).astype(o_ref.dtype)

            w_loc = btot - bcum_col + ig_col
            m_loc = jnp.max(w_loc, axis=0, keepdims=True)
            a_loc = jnp.exp(w_loc - m_loc)
            ct_loc = jnp.dot(kf.T.astype(BF16), (vf * a_loc).astype(BF16),
                             preferred_element_type=F32)
            n_loc = jnp.sum(a_loc * kf, axis=0, keepdims=True)
            m_new = jnp.maximum(btot + m_prev, m_loc)
            a_old = jnp.exp(btot + m_prev - m_new)
            a_new = jnp.exp(m_loc - m_new)
            ct_s[st] = a_old * ct_prev + a_new * ct_loc
            n_s[st] = a_old * n_prev + a_new * n_loc
            m_s[st] = jnp.broadcast_to(m_new, (1, LANES))


def _mlstm(h, gates, gate_bias_row, batch, seq):
    t = h.shape[0]
    nh, dk, dv, cl = MLSTM_HEADS, MLSTM_QK_DIM, MLSTM_V_DIM, MLSTM_CHUNK
    h3 = h.reshape(batch, seq, h.shape[1])
    g3 = gates.reshape(batch, seq, LANES)
    cols = lambda col: (lambda c: (0, c, col))
    out = pl.pallas_call(
        _mlstm_kernel,
        out_shape=jax.ShapeDtypeStruct((batch, seq, nh * dv), BF16),
        grid=(seq // cl,),
        in_specs=[pl.BlockSpec((batch, cl, nh * dk), cols(0)),
                  pl.BlockSpec((batch, cl, nh * dk), cols(1)),
                  pl.BlockSpec((batch, cl, nh * dv), cols(1)),
                  pl.BlockSpec((batch, cl, nh * dv), cols(2)),
                  pl.BlockSpec((batch, cl, LANES), cols(0)),
                  pl.BlockSpec((1, LANES), lambda c: (0, 0))],
        out_specs=pl.BlockSpec((batch, cl, nh * dv), cols(0)),
        scratch_shapes=[pltpu.VMEM((batch * nh, dk, dv), F32),
                        pltpu.VMEM((batch * nh, 1, dk), F32),
                        pltpu.VMEM((batch * nh, 1, LANES), F32)],
        compiler_params=_params("arbitrary"),
        name="mlstm",
    )(h3, h3, h3, h3, g3, gate_bias_row)
    return out.reshape(t, nh * dv)


_E1_GROUPS = PEER_NKEYS // SUBLANES
_E1_ROWS_PER_PASS = 2
_BF16_PER_WORD = 2


def _pack_bf16(x):
    return pltpu.bitcast(x.astype(BF16), jnp.uint32)


def _unpack_bf16(words):
    return pltpu.bitcast(words, BF16)


def _bf16_rows(row, n_rows):
    tile_rows = SUBLANES * _BF16_PER_WORD
    return jnp.tile(jnp.broadcast_to(row, (tile_rows, row.shape[1])).astype(BF16),
                    (n_rows // tile_rows, 1))


def _sorting_network(n):
    pairs = []
    p = 1
    while p < n:
        k = p
        while k >= 1:
            for j in range(k % p, n - k, 2 * k):
                for i in range(min(k, n - j - k)):
                    if (i + j) // (2 * p) == (i + j + k) // (2 * p):
                        pairs.append((i + j, i + j + k))
            k //= 2
        p *= 2
    return pairs


def _pop_sorted_columns(levels, k, extra=None, on_pop=None):
    levels = list(levels)
    depth = len(levels)
    vals = []
    for t in range(k):
        top = levels[0] if extra is None else jnp.maximum(levels[0], extra)
        mx = jnp.max(top, axis=0, keepdims=True)
        vals.append(mx)
        if on_pop is not None:
            on_pop(mx)
        hit = levels[0] == mx
        if extra is not None:
            extra = jnp.where(extra == mx, NEG_INF, extra)
        for lv in range(min(depth, k - t)):
            below = levels[lv + 1] if lv + 1 < depth else NEG_INF
            levels[lv] = jnp.where(hit, below, levels[lv])
    return vals


def _count_passing(rows, passes):
    def probe(bits, index, step):
        if not bits:
            return rows[index + step - 1]
        (mask, weight), rest = bits[0], bits[1:]
        return jnp.where(mask, probe(rest, index + weight, step), probe(rest, index, step))

    bits = []
    count = jnp.where(passes(rows[-1]), 1.0, 0.0)
    step = len(rows) // 2
    while step >= 1:
        mask = passes(probe(bits, 0, step))
        bits.append((mask, step))
        count = count + jnp.where(mask, float(step), 0.0)
        step //= 2
    return count


def _top_values(s, k):
    levels = [s[i * SUBLANES:(i + 1) * SUBLANES] for i in range(s.shape[0] // SUBLANES)]
    for i, j in _sorting_network(len(levels)):
        levels[i], levels[j] = jnp.maximum(levels[i], levels[j]), jnp.minimum(levels[i], levels[j])
    return _pop_sorted_columns(levels, k)


def _peer_topk_kernel(q_ref, keys_ref, r2_ref, p2_ref, n_ref, p1_ref, s1_s, s2_s):
    kk = PEER_TOPK
    assert kk == 2 * SUBLANES
    half = PEER_QDIM // 2
    tb = q_ref.shape[0]
    q = q_ref[...]
    s1_s[...] = lax.dot_general(keys_ref[0, 0], q[:, :half], _TRANS_B, preferred_element_type=F32)
    s2_s[...] = lax.dot_general(keys_ref[0, 1], q[:, half:], _TRANS_B, preferred_element_type=F32)

    def chunk(c):
        lanes = pl.ds(pl.multiple_of(c * LANES, LANES), LANES)
        s1 = s1_s[:, lanes]
        s2 = s2_s[:, lanes]
        v1 = _top_values(s1, kk)
        v2 = _top_values(s2, kk)
        v2_lo = jnp.concatenate(v2[:SUBLANES], axis=0)
        v2_hi = jnp.concatenate(v2[SUBLANES:], axis=0)
        m = v1[0] + v2[0]
        exps = []
        sums = _pop_sorted_columns([v1[a] + v2_lo for a in range(kk)], kk, extra=v1[0] + v2_hi,
                                   on_pop=lambda mx: exps.append(jnp.exp(mx - m)))
        thr = sums[-1]
        den = functools.reduce(lambda a, b: a + b, exps)
        p2_ref[0, :, lanes] = _pack_bf16(jnp.exp(s2 - v2[0]) / den)
        rank2 = _count_passing(v2, lambda t: t > s2)
        n = _count_passing(v2, lambda t: s1 + t >= thr)
        r2_ref[0, :, lanes] = _pack_bf16(rank2)
        p1 = jnp.exp(s1 - v1[0])
        for g in range(_E1_GROUPS):
            rows = slice(g * SUBLANES, (g + 1) * SUBLANES)
            n_ref[g, 0, :, lanes] = n[rows]
            p1_ref[g, 0, :, lanes] = p1[rows]

    def chunk_pair(i, carry):
        chunk(2 * i)
        chunk(2 * i + 1)
        return carry

    lax.fori_loop(0, tb // (2 * LANES), chunk_pair, 0)


def _peer_topk(qb, keys_b, tb=512):
    t = qb.shape[0]
    nh, nk = PEER_HEADS, PEER_NKEYS
    e2_spec = pl.BlockSpec((1, nk // _BF16_PER_WORD, tb), lambda i, h: (h, 0, i))
    e1_spec = pl.BlockSpec((_E1_GROUPS, 1, SUBLANES, tb), lambda i, h: (0, h, 0, i))
    return pl.pallas_call(
        _peer_topk_kernel,
        out_shape=(jax.ShapeDtypeStruct((nh, nk // _BF16_PER_WORD, t), jnp.uint32),
                   jax.ShapeDtypeStruct((nh, nk // _BF16_PER_WORD, t), jnp.uint32),
                   jax.ShapeDtypeStruct((_E1_GROUPS, nh, SUBLANES, t), F32),
                   jax.ShapeDtypeStruct((_E1_GROUPS, nh, SUBLANES, t), F32)),
        grid=(t // tb, nh),
        in_specs=[pl.BlockSpec((tb, PEER_QDIM), lambda i, h: (i, h)),
                  pl.BlockSpec((1, 2, nk, PEER_QDIM // 2), lambda i, h: (h, 0, 0, 0))],
        out_specs=(e2_spec, e2_spec, e1_spec, e1_spec),
        scratch_shapes=[pltpu.VMEM((nk, tb), F32), pltpu.VMEM((nk, tb), F32)],
        compiler_params=_params("parallel", "parallel"),
        name="peer_topk",
    )(qb, keys_b)


def _gelu_exact(x):
    return 0.5 * x * (1.0 + lax.erf(x * (2.0 ** -0.5)))


def _zero_after(v):
    bits = pltpu.bitcast(v, jnp.uint32)
    return pltpu.bitcast((bits >> 16) >> 16, F32)[0:1, :]


def _peer_tile_activations(hid_s, act_s, group, row0, r2_ref, p2_ref, n_ref, p1_ref, after):
    nh, nk = PEER_HEADS, PEER_NKEYS
    te, tb = hid_s.shape
    zero = jnp.zeros((nk, LANES), BF16)
    for c in range(tb // LANES):
        lanes = slice(c * LANES, (c + 1) * LANES)
        for le1_0 in range(0, te // nk, _E1_ROWS_PER_PASS):
            le1s = range(le1_0, le1_0 + _E1_ROWS_PER_PASS)
            w = [zero for _ in le1s]
            for h in range(nh):
                rank2 = _unpack_bf16(r2_ref[h, :, lanes])
                p2 = _unpack_bf16(p2_ref[h, :, lanes])
                for j, le1 in enumerate(le1s):
                    r = row0 + le1
                    n_rows = _bf16_rows(n_ref[group, h, r:r + 1, lanes] + after, nk)
                    p1_rows = _bf16_rows(p1_ref[group, h, r:r + 1, lanes], nk)
                    w[j] = w[j] + jnp.where(rank2 < n_rows, p2 * p1_rows, zero)
            for j, le1 in enumerate(le1s):
                rows = slice(le1 * nk, (le1 + 1) * nk)
                act = w[j] * _gelu_exact(hid_s[rows, lanes]).astype(BF16)
                act_s[rows, lanes] = act
            after = _zero_after(act[0:2 * SUBLANES, :])
    return after


def _peer_dense_kernel(x_ref, u_ref, vt_ref, r2_ref, p2_ref, n_ref, p1_ref, o_ref,
                       acc_s, hid0_s, hid1_s, act0_s, act1_s):
    te = hid0_s.shape[0]
    rows_per_tile = te // PEER_NKEYS
    s = pl.program_id(1)
    last = pl.num_programs(1) - 1
    no_dep = jnp.zeros((1, LANES), F32)

    def hidden(hid_s, half, x):
        hid_s[...] = lax.dot_general(u_ref[half * te:(half + 1) * te, :], x, _TRANS_B,
                                     preferred_element_type=F32)

    def accumulate(act_s, half):
        acc_s[...] += jnp.dot(vt_ref[:, half * te:(half + 1) * te], act_s[...],
                              preferred_element_type=F32)

    def activations(hid_s, act_s, group, row0, after):
        return _peer_tile_activations(hid_s, act_s, group, row0, r2_ref, p2_ref, n_ref, p1_ref,
                                      after)

    @pl.when(s == 0)
    def _():
        acc_s[...] = jnp.zeros(acc_s.shape, F32)
        x = x_ref[...]
        hidden(hid0_s, 0, x)
        hidden(hid1_s, 1, x)
        activations(hid0_s, act0_s, 0, 0, no_dep)

    @pl.when((s > 0) & (s < last))
    def _():
        x = x_ref[...]
        after = activations(hid1_s, act1_s, s - 1, rows_per_tile, no_dep)
        accumulate(act0_s, 0)
        hidden(hid0_s, 0, x)
        hidden(hid1_s, 1, x)
        activations(hid0_s, act0_s, s, 0, after)
        accumulate(act1_s, 1)

    @pl.when(s == last)
    def _():
        activations(hid1_s, act1_s, s - 1, rows_per_tile, no_dep)
        accumulate(act0_s, 0)
        accumulate(act1_s, 1)
        o_ref[...] = acc_s[...].T


def _group_transpose_kernel(v_ref, o_ref):
    o_ref[...] = v_ref[...].T.astype(o_ref.dtype)


def _group_transpose_bf16(v, group_rows, tc=512):
    ne, dm = v.shape
    return pl.pallas_call(
        _group_transpose_kernel,
        out_shape=jax.ShapeDtypeStruct((ne // group_rows, dm, group_rows), BF16),
        grid=(ne // group_rows, dm // tc),
        in_specs=[pl.BlockSpec((group_rows, tc), lambda g, j: (g, j))],
        out_specs=pl.BlockSpec((None, tc, group_rows), lambda g, j: (g, j, 0)),
        compiler_params=_params("parallel", "parallel"),
        name="group_transpose",
    )(v)


def _peer_dense(xb, u_b, v, r2, p2, n_rows, p1_rows, tb=512):
    t, dm = xb.shape
    nh, nk = PEER_HEADS, PEER_NKEYS
    ne = u_b.shape[0]
    te = SUBLANES * nk // 2
    n_pairs = ne // (2 * te)
    assert n_pairs == _E1_GROUPS
    vt_b = _group_transpose_bf16(v, 2 * te)
    e2_spec = pl.BlockSpec((nh, nk // _BF16_PER_WORD, tb), lambda i, s: (0, 0, i))
    e1_spec = pl.BlockSpec((_E1_GROUPS, nh, SUBLANES, tb), lambda i, s: (0, 0, 0, i))
    return pl.pallas_call(
        _peer_dense_kernel,
        out_shape=jax.ShapeDtypeStruct((t, dm), F32),
        grid=(t // tb, n_pairs + 1),
        in_specs=[pl.BlockSpec((tb, dm), lambda i, s: (i, 0)),
                  pl.BlockSpec((2 * te, dm), lambda i, s: (jnp.minimum(s, n_pairs - 1), 0)),
                  pl.BlockSpec((None, dm, 2 * te), lambda i, s: (jnp.maximum(s - 1, 0), 0, 0)),
                  e2_spec, e2_spec, e1_spec, e1_spec],
        out_specs=pl.BlockSpec((tb, dm), lambda i, s: (i, 0)),
        scratch_shapes=[pltpu.VMEM((dm, tb), F32),
                        pltpu.VMEM((te, tb), F32), pltpu.VMEM((te, tb), F32),
                        pltpu.VMEM((te, tb), BF16), pltpu.VMEM((te, tb), BF16)],
        compiler_params=_params("parallel", "arbitrary"),
        name="peer_dense",
    )(xb, u_b, vt_b, r2, p2, n_rows, p1_rows)


def _peer(xb, w_q_b, keys_b, u_b, v):
    qb = _matmul(xb, w_q_b, BF16)
    r2, p2, n_rows, p1_rows = _peer_topk(qb, keys_b)
    return _peer_dense(xb, u_b, v, r2, p2, n_rows, p1_rows)


def _rope_tables(positions):
    half = ATT_HEAD_DIM // 2
    inv_freq = jnp.power(ROPE_THETA, -2.0 * jnp.arange(half, dtype=F32) / ATT_HEAD_DIM)
    ang = positions.astype(F32).reshape(-1, 1) * inv_freq
    cos, sin = jnp.cos(ang), jnp.sin(ang)
    return jnp.concatenate([cos, cos], axis=-1), jnp.concatenate([-sin, sin], axis=-1)


def kernel(x, p, positions, even_w_in, even_conv_w, even_w_out, odd_w_in, odd_gate_bias,
           odd_w_out, peer_w_q, peer_sub_keys, peer_u, peer_v, ple_w_proj, ple_w_gate,
           ln_gain, ln_bias):
    batch, seq, dm = x.shape
    t = batch * seq
    assert dm == D_MODEL and seq % ATT_TOKENS == 0 and seq % MLSTM_CHUNK == 0
    xf = x.reshape(t, dm)
    xb = xf.astype(BF16)
    cosf, sinf = _rope_tables(positions)

    for layer in range(DEPTH):
        j = layer // 2
        if layer % 2 == 0:
            h = _matmul(xb, even_w_in[j].astype(BF16), F32)
            y_conv = _conv_gate(h, even_conv_w[j], seq)
            y_att = _dilated_attention(h, cosf, sinf, batch, seq)
            mixed = [y_conv, y_att]
            w_out = even_w_out[j]
        else:
            w_in = odd_w_in[j]
            h = _matmul(xb, w_in[:, :ODD_MAIN].astype(BF16), F32)
            n_gates = 2 * MLSTM_HEADS
            w_gates = jnp.pad(w_in[:, ODD_MAIN:], ((0, 0), (0, LANES - n_gates))).astype(BF16)
            gates = _matmul(xb, w_gates, F32)
            bias_row = jnp.pad(odd_gate_bias[j].reshape(1, n_gates), ((0, 0), (0, LANES - n_gates)))
            mixed = [_mlstm(h, gates, bias_row, batch, seq)]
            w_out = odd_w_out[j]
        xf, xb = _matmul_residual_ln(mixed, w_out.astype(BF16), xf,
                                     ln_gain[layer, 0], ln_bias[layer, 0])
        ffn = _peer(xb, peer_w_q[layer].astype(BF16), peer_sub_keys[layer].astype(BF16),
                    peer_u[layer].astype(BF16), peer_v[layer])
        xf, xb = _ple_residual_ln(xf, xb, ffn, p[layer].reshape(t, PLE_DIM),
                                  ple_w_gate[layer].astype(BF16), ple_w_proj[layer].astype(BF16),
                                  ln_gain[layer, 1], ln_bias[layer, 1])
    return xf.reshape(batch, seq, dm)
```

```python
import functools

import jax
import jax.numpy as jnp
from jax import lax
from jax.experimental import pallas as pl
from jax.experimental.pallas import tpu as pltpu

D_MODEL = 2048
DEPTH = 4
CONV_CH = D_MODEL // 2
CONV_WIDTH = 3
ATT_HEADS = 8
ATT_HEAD_DIM = 128
ATT_WIDTH = ATT_HEADS * ATT_HEAD_DIM
DILATED_PAIRS = ((128, 1), (512, 4), (2048, 16))
ROPE_THETA = 10000.0
MLSTM_HEADS = 8
MLSTM_QK_DIM = 128
MLSTM_V_DIM = D_MODEL // MLSTM_HEADS
MLSTM_CHUNK = 128
ODD_MAIN = 2 * MLSTM_HEADS * MLSTM_QK_DIM + 2 * MLSTM_HEADS * MLSTM_V_DIM
PEER_HEADS = 8
PEER_NKEYS = 128
PEER_QDIM = 256
PEER_TOPK = 16
PLE_DIM = 256
ALPHA = (2.0 * DEPTH) ** 0.25
LN_EPS = 1e-5

LANES = 128
SUBLANES = 8
MXU_TILE = 256
VMEM_LIMIT_BYTES = 56 * 1024 * 1024

ATT_BLOCK = 128
ATT_SPAN = 128
ATT_TOKENS = ATT_BLOCK * max(d for _, d in DILATED_PAIRS)

NEG_INF = float("-inf")
F32 = jnp.float32
BF16 = jnp.bfloat16

_TRANS_B = (((1,), (1,)), ((), ()))


def _params(*semantics):
    return pltpu.CompilerParams(dimension_semantics=semantics,
                                vmem_limit_bytes=VMEM_LIMIT_BYTES)


def _layer_norm_rows(y, gain, bias):
    mu = jnp.mean(y, axis=-1, keepdims=True)
    yc = y - mu
    var = jnp.mean(yc * yc, axis=-1, keepdims=True)
    return yc * lax.rsqrt(var + LN_EPS) * gain + bias


def _mm_kernel(a_ref, w_ref, o_ref):
    o_ref[...] = jnp.dot(a_ref[...], w_ref[...],
                         preferred_element_type=F32).astype(o_ref.dtype)


def _matmul(a, w, out_dtype, tm=512, tn=None):
    m, k = a.shape
    n = w.shape[1]
    tn = min(n, 2048) if tn is None else tn
    return pl.pallas_call(
        _mm_kernel,
        out_shape=jax.ShapeDtypeStruct((m, n), out_dtype),
        grid=(n // tn, m // tm),
        in_specs=[pl.BlockSpec((tm, k), lambda j, i: (i, 0)),
                  pl.BlockSpec((k, tn), lambda j, i: (0, j))],
        out_specs=pl.BlockSpec((tm, tn), lambda j, i: (i, j)),
        compiler_params=_params("parallel", "parallel"),
        name="matmul",
    )(a, w)


def _mm_ln_kernel(*refs, n_a):
    a_refs = refs[:n_a]
    w_ref, res_ref, g_ref, b_ref, o_ref, ob_ref = refs[n_a:]
    for r0 in range(0, res_ref.shape[0], MXU_TILE):
        rows = slice(r0, r0 + MXU_TILE)
        acc = None
        off = 0
        for a_ref in a_refs:
            k = a_ref.shape[1]
            part = jnp.dot(a_ref[rows, :], w_ref[off:off + k, :], preferred_element_type=F32)
            acc = part if acc is None else acc + part
            off += k
        y = ALPHA * res_ref[rows, :] + acc
        out = _layer_norm_rows(y, g_ref[...], b_ref[...])
        o_ref[rows, :] = out
        ob_ref[rows, :] = out.astype(BF16)


def _matmul_residual_ln(a_list, w, res, gain, bias, tm=512):
    m = res.shape[0]
    n = w.shape[1]
    in_specs = [pl.BlockSpec((tm, a.shape[1]), lambda i: (i, 0)) for a in a_list]
    in_specs += [pl.BlockSpec(w.shape, lambda i: (0, 0)),
                 pl.BlockSpec((tm, n), lambda i: (i, 0)),
                 pl.BlockSpec((1, n), lambda i: (0, 0)),
                 pl.BlockSpec((1, n), lambda i: (0, 0))]
    return pl.pallas_call(
        functools.partial(_mm_ln_kernel, n_a=len(a_list)),
        out_shape=(jax.ShapeDtypeStruct((m, n), F32), jax.ShapeDtypeStruct((m, n), BF16)),
        grid=(m // tm,),
        in_specs=in_specs,
        out_specs=(pl.BlockSpec((tm, n), lambda i: (i, 0)),
                   pl.BlockSpec((tm, n), lambda i: (i, 0))),
        compiler_params=_params("parallel"),
        name="matmul_residual_ln",
    )(*a_list, w, res, gain.reshape(1, n), bias.reshape(1, n))


def _ple_ln_kernel(x_ref, xb_ref, ffn_ref, p_ref, wg_ref, wp_ref, g_ref, b_ref, o_ref, ob_ref):
    for r0 in range(0, x_ref.shape[0], MXU_TILE):
        rows = slice(r0, r0 + MXU_TILE)
        gate = jax.nn.sigmoid(jnp.dot(xb_ref[rows, :], wg_ref[...], preferred_element_type=F32))
        proj = jnp.dot(p_ref[rows, :].astype(BF16), wp_ref[...], preferred_element_type=F32)
        y = ALPHA * x_ref[rows, :] + ffn_ref[rows, :] + gate * proj
        out = _layer_norm_rows(y, g_ref[...], b_ref[...])
        o_ref[rows, :] = out
        ob_ref[rows, :] = out.astype(BF16)


def _ple_residual_ln(x, xb, ffn, p, wg, wp, gain, bias, tm=512):
    m, n = x.shape
    row = lambda i: (i, 0)
    const = lambda i: (0, 0)
    return pl.pallas_call(
        _ple_ln_kernel,
        out_shape=(jax.ShapeDtypeStruct((m, n), F32), jax.ShapeDtypeStruct((m, n), BF16)),
        grid=(m // tm,),
        in_specs=[pl.BlockSpec((tm, n), row), pl.BlockSpec((tm, n), row),
                  pl.BlockSpec((tm, n), row), pl.BlockSpec((tm, p.shape[1]), row),
                  pl.BlockSpec(wg.shape, const), pl.BlockSpec(wp.shape, const),
                  pl.BlockSpec((1, n), const), pl.BlockSpec((1, n), const)],
        out_specs=(pl.BlockSpec((tm, n), row), pl.BlockSpec((tm, n), row)),
        compiler_params=_params("parallel"),
        name="ple_residual_ln",
    )(x, xb, ffn, p, wg, wp, gain.reshape(1, n), bias.reshape(1, n))


def _conv_gate_kernel(gb_ref, gc_ref, xa_ref, w_ref, o_ref, buf, *, steps_per_seq):
    tm = gb_ref.shape[0]
    halo = SUBLANES

    @pl.when(pl.program_id(0) % steps_per_seq == 0)
    def _():
        buf[0:halo, :] = jnp.zeros((halo, buf.shape[1]), F32)

    u = gc_ref[...] * xa_ref[...]
    buf[halo:halo + tm, :] = u
    w = w_ref[...]
    conv = (w[0:1, :] * buf[halo - 2:halo - 2 + tm, :]
            + w[1:2, :] * buf[halo - 1:halo - 1 + tm, :]
            + w[2:3, :] * u)
    o_ref[...] = (gb_ref[...] * conv).astype(o_ref.dtype)
    buf[0:halo, :] = buf[tm:tm + halo, :]


def _conv_gate(h, conv_w, seq, tm=512):
    t = h.shape[0]
    ch = CONV_CH
    return pl.pallas_call(
        functools.partial(_conv_gate_kernel, steps_per_seq=seq // tm),
        out_shape=jax.ShapeDtypeStruct((t, ch), BF16),
        grid=(t // tm,),
        in_specs=[pl.BlockSpec((tm, ch), lambda i: (i, 0)),
                  pl.BlockSpec((tm, ch), lambda i: (i, 1)),
                  pl.BlockSpec((tm, ch), lambda i: (i, 2)),
                  pl.BlockSpec((CONV_WIDTH, ch), lambda i: (0, 0))],
        out_specs=pl.BlockSpec((tm, ch), lambda i: (i, 0)),
        scratch_shapes=[pltpu.VMEM((tm + SUBLANES, ch), F32)],
        compiler_params=_params("arbitrary"),
        name="conv_gate",
    )(h, h, h, conv_w)


def _rope(t, cosf, sinf):
    return t * cosf + pltpu.roll(t, ATT_HEAD_DIM // 2, 1) * sinf


def _attn_kernel(q_ref, kc_ref, kp_ref, vc_ref, vp_ref, cc_ref, sc_ref, cp_ref, sp_ref,
                 o_ref, q_s, k_s, v_s, acc_s, m_s, l_s):
    j = pl.program_id(2)
    bl = ATT_TOKENS
    blk = ATT_BLOCK
    cc = cc_ref[...]
    sc = sc_ref[...]
    q_s[...] = _rope(q_ref[...], cc, sc)
    k_s[bl:2 * bl, :] = _rope(kc_ref[...], cc, sc)
    k_s[0:bl, :] = _rope(kp_ref[...], cp_ref[...], sp_ref[...])
    v_s[bl:2 * bl, :] = vc_ref[...]
    v_s[0:bl, :] = vp_ref[...]

    qi = lax.broadcasted_iota(jnp.int32, (blk, 2 * blk), 0)
    kj = lax.broadcasted_iota(jnp.int32, (blk, 2 * blk), 1)
    dist = qi + blk - kj
    band = (dist >= 0) & (dist <= ATT_SPAN)
    band_first = band & (kj >= jnp.where(j > 0, 0, blk))
    scale = ATT_HEAD_DIM ** -0.5

    for branch, (_, d) in enumerate(DILATED_PAIRS):
        for g in range(bl // (blk * d)):
            for r in range(d):
                q_rows = pl.ds(g * blk * d + r, blk, stride=d) if d > 1 else pl.ds(g * blk, blk)
                k_start = bl + (g - 1) * blk * d + r
                k_rows = pl.ds(k_start, 2 * blk, stride=d) if d > 1 else pl.ds(k_start, 2 * blk)
                qs = q_s[q_rows, :].astype(BF16)
                ks = k_s[k_rows, :].astype(BF16)
                vs = v_s[k_rows, :].astype(BF16)
                s = lax.dot_general(qs, ks, _TRANS_B, preferred_element_type=F32) * scale
                s = jnp.where(band_first if g == 0 else band, s, NEG_INF)
                m_b = jnp.max(s, axis=-1, keepdims=True)
                e = jnp.exp(s - m_b)
                l_b = jnp.sum(e, axis=-1, keepdims=True)
                o_b = jnp.dot(e.astype(BF16), vs, preferred_element_type=F32)
                if branch == 0:
                    acc_s[q_rows, :] = o_b
                    m_s[q_rows, :] = jnp.broadcast_to(m_b, (blk, LANES))
                    l_s[q_rows, :] = jnp.broadcast_to(l_b, (blk, LANES))
                else:
                    m_old = m_s[q_rows, :]
                    m_new = jnp.maximum(m_old, m_b)
                    a_old = jnp.exp(m_old - m_new)
                    a_new = jnp.exp(m_b - m_new)
                    acc_s[q_rows, :] = a_old * acc_s[q_rows, :] + a_new * o_b
                    l_s[q_rows, :] = a_old * l_s[q_rows, :] + a_new * l_b
                    m_s[q_rows, :] = m_new

    o_ref[...] = (acc_s[...] / l_s[...]).astype(o_ref.dtype)


def _dilated_attention(h, cosf, sinf, batch, seq):
    t = h.shape[0]
    bl = ATT_TOKENS
    nj = seq // bl
    hd = ATT_HEAD_DIM
    q_col = 3 * CONV_CH // hd
    k_col = q_col + ATT_HEADS
    v_col = k_col + ATT_HEADS

    def cur(col):
        return lambda b, hh, j: (b * nj + j, col + hh)

    def prev(col):
        return lambda b, hh, j: (b * nj + jnp.maximum(j - 1, 0), col + hh)

    rows_cur = lambda b, hh, j: (b * nj + j, 0)
    rows_prev = lambda b, hh, j: (b * nj + jnp.maximum(j - 1, 0), 0)
    blk = (bl, hd)
    return pl.pallas_call(
        _attn_kernel,
        out_shape=jax.ShapeDtypeStruct((t, ATT_WIDTH), BF16),
        grid=(batch, ATT_HEADS, nj),
        in_specs=[pl.BlockSpec(blk, cur(q_col)),
                  pl.BlockSpec(blk, cur(k_col)), pl.BlockSpec(blk, prev(k_col)),
                  pl.BlockSpec(blk, cur(v_col)), pl.BlockSpec(blk, prev(v_col)),
                  pl.BlockSpec(blk, rows_cur), pl.BlockSpec(blk, rows_cur),
                  pl.BlockSpec(blk, rows_prev), pl.BlockSpec(blk, rows_prev)],
        out_specs=pl.BlockSpec(blk, lambda b, hh, j: (b * nj + j, hh)),
        scratch_shapes=[pltpu.VMEM((bl, hd), F32), pltpu.VMEM((2 * bl, hd), F32),
                        pltpu.VMEM((2 * bl, hd), F32), pltpu.VMEM((bl, hd), F32),
                        pltpu.VMEM((bl, LANES), F32), pltpu.VMEM((bl, LANES), F32)],
        compiler_params=_params("parallel", "parallel", "parallel"),
        name="dilated_attention",
    )(h, h, h, h, h, cosf, sinf, cosf, sinf)


def _log_sigmoid(x):
    return jnp.minimum(x, 0.0) - jnp.log(1.0 + jnp.exp(-jnp.abs(x)))


def _mlstm_kernel(q_ref, k_ref, v_ref, og_ref, g_ref, bias_ref, o_ref, ct_s, n_s, m_s):
    nh, dk, dv, cl = MLSTM_HEADS, MLSTM_QK_DIM, MLSTM_V_DIM, MLSTM_CHUNK
    n_seq = q_ref.shape[0]

    @pl.when(pl.program_id(0) == 0)
    def _():
        ct_s[...] = jnp.zeros(ct_s.shape, F32)
        n_s[...] = jnp.zeros(n_s.shape, F32)
        m_s[...] = jnp.zeros(m_s.shape, F32)

    row = lax.broadcasted_iota(jnp.int32, (cl, cl), 0)
    col = lax.broadcasted_iota(jnp.int32, (cl, cl), 1)
    causal = col <= row
    ones = jnp.ones((cl, LANES), BF16)

    for b in range(n_seq):
        gates = g_ref[b] + bias_ref[...]
        gates_t = gates.T
        for hd in range(nh):
            st = b * nh + hd
            ig_col = gates[:, hd:hd + 1]
            ig_row = gates_t[hd:hd + 1, :]
            lf_col = _log_sigmoid(gates[:, nh + hd:nh + hd + 1])
            lf_row = _log_sigmoid(gates_t[nh + hd:nh + hd + 1, :])
            bcum_col = jnp.sum(jnp.where(causal, lf_row, 0.0), axis=1, keepdims=True)
            bcum_row = jnp.sum(jnp.where(row <= col, lf_col, 0.0), axis=0, keepdims=True)
            btot = bcum_col[cl - 1:cl, :]

            qb = q_ref[b, :, hd * dk:(hd + 1) * dk].astype(BF16)
            kf = k_ref[b, :, hd * dk:(hd + 1) * dk] * (dk ** -0.5)
            vf = v_ref[b, :, hd * dv:(hd + 1) * dv]
            kb = kf.astype(BF16)
            ct_prev = ct_s[st]
            n_prev = n_s[st]
            m_prev = m_s[st][:, 0:1]

            dmat = jnp.where(causal, bcum_col - bcum_row + ig_row, NEG_INF)
            g_inter = bcum_col + m_prev
            m_t = jnp.maximum(g_inter, jnp.max(dmat, axis=1, keepdims=True))
            qk = lax.dot_general(qb, kb, _TRANS_B, preferred_element_type=F32)
            sc = (qk * jnp.exp(dmat - m_t)).astype(BF16)
            a_inter = jnp.exp(g_inter - m_t)
            num = (a_inter * jnp.dot(qb, ct_prev.astype(BF16), preferred_element_type=F32)
                   + jnp.dot(sc, vf.astype(BF16), preferred_element_type=F32))
            n_rows = jnp.broadcast_to(n_prev, (LANES, dk)).astype(BF16)
            den = (a_inter * lax.dot_general(qb, n_rows, _TRANS_B, preferred_element_type=F32)
                   + jnp.dot(sc, ones, preferred_element_type=F32))
            scale = 1.0 / jnp.maximum(jnp.abs(den), jnp.exp(-m_t))
            hh = num * jnp.tile(scale, (1, dv // LANES))
            o_ref[b, :, hd * dv:(hd + 1) * dv] = (
                jax.nn.sigmoid(og_ref[b, :, hd * dv:(hd + 1) * dv]) * hh).astype(o_ref.dtype)

            w_loc = btot - bcum_col + ig_col
            m_loc = jnp.max(w_loc, axis=0, keepdims=True)
            a_loc = jnp.exp(w_loc - m_loc)
            ct_loc = jnp.dot(kf.T.astype(BF16), (vf * a_loc).astype(BF16),
                             preferred_element_type=F32)
            n_loc = jnp.sum(a_loc * kf, axis=0, keepdims=True)
            m_new = jnp.maximum(btot + m_prev, m_loc)
            a_old = jnp.exp(btot + m_prev - m_new)
            a_new = jnp.exp(m_loc - m_new)
            ct_s[st] = a_old * ct_prev + a_new * ct_loc
            n_s[st] = a_old * n_prev + a_new * n_loc
            m_s[st] = jnp.broadcast_to(m_new, (1, LANES))


def _mlstm(h, gates, gate_bias_row, batch, seq):
    t = h.shape[0]
    nh, dk, dv, cl = MLSTM_HEADS, MLSTM_QK_DIM, MLSTM_V_DIM, MLSTM_CHUNK
    h3 = h.reshape(batch, seq, h.shape[1])
    g3 = gates.reshape(batch, seq, LANES)
    cols = lambda col: (lambda c: (0, c, col))
    out = pl.pallas_call(
        _mlstm_kernel,
        out_shape=jax.ShapeDtypeStruct((batch, seq, nh * dv), BF16),
        grid=(seq // cl,),
        in_specs=[pl.BlockSpec((batch, cl, nh * dk), cols(0)),
                  pl.BlockSpec((batch, cl, nh * dk), cols(1)),
                  pl.BlockSpec((batch, cl, nh * dv), cols(1)),
                  pl.BlockSpec((batch, cl, nh * dv), cols(2)),
                  pl.BlockSpec((batch, cl, LANES), cols(0)),
                  pl.BlockSpec((1, LANES), lambda c: (0, 0))],
        out_specs=pl.BlockSpec((batch, cl, nh * dv), cols(0)),
        scratch_shapes=[pltpu.VMEM((batch * nh, dk, dv), F32),
                        pltpu.VMEM((batch * nh, 1, dk), F32),
                        pltpu.VMEM((batch * nh, 1, LANES), F32)],
        compiler_params=_params("arbitrary"),
        name="mlstm",
    )(h3, h3, h3, h3, g3, gate_bias_row)
    return out.reshape(t, nh * dv)


_E1_GROUPS = PEER_NKEYS // SUBLANES
_E1_ROWS_PER_PASS = 2
_TOPK_CHUNKS_PER_TRIP = 4
_BF16_PER_WORD = 2


def _pack_bf16(x):
    return pltpu.bitcast(x.astype(BF16), jnp.uint32)


def _unpack_bf16(words):
    return pltpu.bitcast(words, BF16)


def _bf16_rows(row, n_rows):
    tile_rows = SUBLANES * _BF16_PER_WORD
    return jnp.tile(jnp.broadcast_to(row, (tile_rows, row.shape[1])).astype(BF16),
                    (n_rows // tile_rows, 1))


def _sorting_network(n):
    pairs = []
    p = 1
    while p < n:
        k = p
        while k >= 1:
            for j in range(k % p, n - k, 2 * k):
                for i in range(min(k, n - j - k)):
                    if (i + j) // (2 * p) == (i + j + k) // (2 * p):
                        pairs.append((i + j, i + j + k))
            k //= 2
        p *= 2
    return pairs


def _pop_sorted_columns(levels, k, extra=None, on_pop=None):
    levels = list(levels)
    depth = len(levels)
    vals = []
    for t in range(k):
        top = levels[0] if extra is None else jnp.maximum(levels[0], extra)
        mx = jnp.max(top, axis=0, keepdims=True)
        vals.append(mx)
        if on_pop is not None:
            on_pop(mx)
        hit = levels[0] == mx
        if extra is not None:
            extra = jnp.where(extra == mx, NEG_INF, extra)
        for lv in range(min(depth, k - t)):
            below = levels[lv + 1] if lv + 1 < depth else NEG_INF
            levels[lv] = jnp.where(hit, below, levels[lv])
    return vals


def _count_passing(rows, passes):
    def probe(bits, index, step):
        if not bits:
            return rows[index + step - 1]
        (mask, weight), rest = bits[0], bits[1:]
        return jnp.where(mask, probe(rest, index + weight, step), probe(rest, index, step))

    bits = []
    count = jnp.where(passes(rows[-1]), 1.0, 0.0)
    step = len(rows) // 2
    while step >= 1:
        mask = passes(probe(bits, 0, step))
        bits.append((mask, step))
        count = count + jnp.where(mask, float(step), 0.0)
        step //= 2
    return count


def _top_values(s, k):
    levels = [s[i * SUBLANES:(i + 1) * SUBLANES] for i in range(s.shape[0] // SUBLANES)]
    for i, j in _sorting_network(len(levels)):
        levels[i], levels[j] = jnp.maximum(levels[i], levels[j]), jnp.minimum(levels[i], levels[j])
    return _pop_sorted_columns(levels, k)


def _peer_topk_kernel(q_ref, keys_ref, r2_ref, p2_ref, n_ref, p1_ref, s1_s, s2_s):
    kk = PEER_TOPK
    assert kk == 2 * SUBLANES
    half = PEER_QDIM // 2
    tb = q_ref.shape[0]
    q = q_ref[...]
    s1_s[...] = lax.dot_general(keys_ref[0, 0], q[:, :half], _TRANS_B, preferred_element_type=F32)
    s2_s[...] = lax.dot_general(keys_ref[0, 1], q[:, half:], _TRANS_B, preferred_element_type=F32)

    def chunk(c):
        lanes = pl.ds(pl.multiple_of(c * LANES, LANES), LANES)
        s1 = s1_s[:, lanes]
        s2 = s2_s[:, lanes]
        v1 = _top_values(s1, kk)
        v2 = _top_values(s2, kk)
        v2_lo = jnp.concatenate(v2[:SUBLANES], axis=0)
        v2_hi = jnp.concatenate(v2[SUBLANES:], axis=0)
        m = v1[0] + v2[0]
        exps = []
        sums = _pop_sorted_columns([v1[a] + v2_lo for a in range(kk)], kk, extra=v1[0] + v2_hi,
                                   on_pop=lambda mx: exps.append(jnp.exp(mx - m)))
        thr = sums[-1]
        den = functools.reduce(lambda a, b: a + b, exps)
        p2_ref[0, :, lanes] = _pack_bf16(jnp.exp(s2 - v2[0]) / den)
        rank2 = _count_passing(v2, lambda t: t > s2)
        n = _count_passing(v2, lambda t: s1 + t >= thr)
        r2_ref[0, :, lanes] = _pack_bf16(rank2)
        p1 = jnp.exp(s1 - v1[0])
        for g in range(_E1_GROUPS):
            rows = slice(g * SUBLANES, (g + 1) * SUBLANES)
            n_ref[g, 0, :, lanes] = n[rows]
            p1_ref[g, 0, :, lanes] = p1[rows]

    def chunk_group(i, carry):
        for j in range(_TOPK_CHUNKS_PER_TRIP):
            chunk(_TOPK_CHUNKS_PER_TRIP * i + j)
        return carry

    lax.fori_loop(0, tb // (_TOPK_CHUNKS_PER_TRIP * LANES), chunk_group, 0)


def _peer_topk(qb, keys_b, tb=512):
    t = qb.shape[0]
    nh, nk = PEER_HEADS, PEER_NKEYS
    e2_spec = pl.BlockSpec((1, nk // _BF16_PER_WORD, tb), lambda i, h: (h, 0, i))
    e1_spec = pl.BlockSpec((_E1_GROUPS, 1, SUBLANES, tb), lambda i, h: (0, h, 0, i))
    return pl.pallas_call(
        _peer_topk_kernel,
        out_shape=(jax.ShapeDtypeStruct((nh, nk // _BF16_PER_WORD, t), jnp.uint32),
                   jax.ShapeDtypeStruct((nh, nk // _BF16_PER_WORD, t), jnp.uint32),
                   jax.ShapeDtypeStruct((_E1_GROUPS, nh, SUBLANES, t), F32),
                   jax.ShapeDtypeStruct((_E1_GROUPS, nh, SUBLANES, t), F32)),
        grid=(t // tb, nh),
        in_specs=[pl.BlockSpec((tb, PEER_QDIM), lambda i, h: (i, h)),
                  pl.BlockSpec((1, 2, nk, PEER_QDIM // 2), lambda i, h: (h, 0, 0, 0))],
        out_specs=(e2_spec, e2_spec, e1_spec, e1_spec),
        scratch_shapes=[pltpu.VMEM((nk, tb), F32), pltpu.VMEM((nk, tb), F32)],
        compiler_params=_params("parallel", "parallel"),
        name="peer_topk",
    )(qb, keys_b)


def _gelu_exact(x):
    return 0.5 * x * (1.0 + lax.erf(x * (2.0 ** -0.5)))


def _zero_after(v):
    bits = pltpu.bitcast(v, jnp.uint32)
    return pltpu.bitcast((bits >> 16) >> 16, F32)[0:1, :]


def _peer_tile_activations(hid_s, act_s, group, row0, r2_ref, p2_ref, n_ref, p1_ref, after):
    nh, nk = PEER_HEADS, PEER_NKEYS
    te, tb = hid_s.shape
    zero = jnp.zeros((nk, LANES), BF16)
    for c in range(tb // LANES):
        lanes = slice(c * LANES, (c + 1) * LANES)
        for le1_0 in range(0, te // nk, _E1_ROWS_PER_PASS):
            le1s = range(le1_0, le1_0 + _E1_ROWS_PER_PASS)
            w = [zero for _ in le1s]
            for h in range(nh):
                rank2 = _unpack_bf16(r2_ref[h, :, lanes])
                p2 = _unpack_bf16(p2_ref[h, :, lanes])
                for j, le1 in enumerate(le1s):
                    r = row0 + le1
                    n_rows = _bf16_rows(n_ref[group, h, r:r + 1, lanes] + after, nk)
                    p1_rows = _bf16_rows(p1_ref[group, h, r:r + 1, lanes], nk)
                    w[j] = w[j] + jnp.where(rank2 < n_rows, p2 * p1_rows, zero)
            for j, le1 in enumerate(le1s):
                rows = slice(le1 * nk, (le1 + 1) * nk)
                act = w[j] * _gelu_exact(hid_s[rows, lanes]).astype(BF16)
                act_s[rows, lanes] = act
            after = _zero_after(act[0:2 * SUBLANES, :])
    return after


def _peer_dense_kernel(x_ref, u_ref, vt_ref, r2_ref, p2_ref, n_ref, p1_ref, o_ref,
                       acc_s, hid0_s, hid1_s, act0_s, act1_s):
    te = hid0_s.shape[0]
    rows_per_tile = te // PEER_NKEYS
    s = pl.program_id(1)
    last = pl.num_programs(1) - 1
    no_dep = jnp.zeros((1, LANES), F32)

    def hidden(hid_s, half, x):
        hid_s[...] = lax.dot_general(u_ref[half * te:(half + 1) * te, :], x, _TRANS_B,
                                     preferred_element_type=F32)

    def accumulate(act_s, half):
        acc_s[...] += jnp.dot(vt_ref[:, half * te:(half + 1) * te], act_s[...],
                              preferred_element_type=F32)

    def activations(hid_s, act_s, group, row0, after):
        return _peer_tile_activations(hid_s, act_s, group, row0, r2_ref, p2_ref, n_ref, p1_ref,
                                      after)

    @pl.when(s == 0)
    def _():
        acc_s[...] = jnp.zeros(acc_s.shape, F32)
        x = x_ref[...]
        hidden(hid0_s, 0, x)
        hidden(hid1_s, 1, x)
        activations(hid0_s, act0_s, 0, 0, no_dep)

    @pl.when((s > 0) & (s < last))
    def _():
        x = x_ref[...]
        after = activations(hid1_s, act1_s, s - 1, rows_per_tile, no_dep)
        accumulate(act0_s, 0)
        hidden(hid0_s, 0, x)
        hidden(hid1_s, 1, x)
        activations(hid0_s, act0_s, s, 0, after)
        accumulate(act1_s, 1)

    @pl.when(s == last)
    def _():
        activations(hid1_s, act1_s, s - 1, rows_per_tile, no_dep)
        accumulate(act0_s, 0)
        accumulate(act1_s, 1)
        o_ref[...] = acc_s[...].T


def _peer_dense(xb, u_b, v_b, r2, p2, n_rows, p1_rows, tb=512):
    t, dm = xb.shape
    nh, nk = PEER_HEADS, PEER_NKEYS
    ne = u_b.shape[0]
    te = SUBLANES * nk // 2
    n_pairs = ne // (2 * te)
    assert n_pairs == _E1_GROUPS
    vt_b = v_b.reshape(n_pairs, 2 * te, dm).transpose(0, 2, 1)
    e2_spec = pl.BlockSpec((nh, nk // _BF16_PER_WORD, tb), lambda i, s: (0, 0, i))
    e1_spec = pl.BlockSpec((_E1_GROUPS, nh, SUBLANES, tb), lambda i, s: (0, 0, 0, i))
    return pl.pallas_call(
        _peer_dense_kernel,
        out_shape=jax.ShapeDtypeStruct((t, dm), F32),
        grid=(t // tb, n_pairs + 1),
        in_specs=[pl.BlockSpec((tb, dm), lambda i, s: (i, 0)),
                  pl.BlockSpec((2 * te, dm), lambda i, s: (jnp.minimum(s, n_pairs - 1), 0)),
                  pl.BlockSpec((None, dm, 2 * te), lambda i, s: (jnp.maximum(s - 1, 0), 0, 0)),
                  e2_spec, e2_spec, e1_spec, e1_spec],
        out_specs=pl.BlockSpec((tb, dm), lambda i, s: (i, 0)),
        scratch_shapes=[pltpu.VMEM((dm, tb), F32),
                        pltpu.VMEM((te, tb), F32), pltpu.VMEM((te, tb), F32),
                        pltpu.VMEM((te, tb), BF16), pltpu.VMEM((te, tb), BF16)],
        compiler_params=_params("parallel", "arbitrary"),
        name="peer_dense",
    )(xb, u_b, vt_b, r2, p2, n_rows, p1_rows)


def _peer(xb, w_q_b, keys_b, u_b, v_b):
    qb = _matmul(xb, w_q_b, BF16)
    r2, p2, n_rows, p1_rows = _peer_topk(qb, keys_b)
    return _peer_dense(xb, u_b, v_b, r2, p2, n_rows, p1_rows)


def _rope_tables(positions):
    half = ATT_HEAD_DIM // 2
    inv_freq = jnp.power(ROPE_THETA, -2.0 * jnp.arange(half, dtype=F32) / ATT_HEAD_DIM)
    ang = positions.astype(F32).reshape(-1, 1) * inv_freq
    cos, sin = jnp.cos(ang), jnp.sin(ang)
    return jnp.concatenate([cos, cos], axis=-1), jnp.concatenate([-sin, sin], axis=-1)


def kernel(x, p, positions, even_w_in, even_conv_w, even_w_out, odd_w_in, odd_gate_bias,
           odd_w_out, peer_w_q, peer_sub_keys, peer_u, peer_v, ple_w_proj, ple_w_gate,
           ln_gain, ln_bias):
    batch, seq, dm = x.shape
    t = batch * seq
    assert dm == D_MODEL and seq % ATT_TOKENS == 0 and seq % MLSTM_CHUNK == 0
    xf = x.reshape(t, dm)
    xb = xf.astype(BF16)
    cosf, sinf = _rope_tables(positions)

    for layer in range(DEPTH):
        j = layer // 2
        if layer % 2 == 0:
            h = _matmul(xb, even_w_in[j].astype(BF16), F32)
            y_conv = _conv_gate(h, even_conv_w[j], seq)
            y_att = _dilated_attention(h, cosf, sinf, batch, seq)
            mixed = [y_conv, y_att]
            w_out = even_w_out[j]
        else:
            w_in = odd_w_in[j]
            h = _matmul(xb, w_in[:, :ODD_MAIN].astype(BF16), F32)
            n_gates = 2 * MLSTM_HEADS
            w_gates = jnp.pad(w_in[:, ODD_MAIN:], ((0, 0), (0, LANES - n_gates))).astype(BF16)
            gates = _matmul(xb, w_gates, F32)
            bias_row = jnp.pad(odd_gate_bias[j].reshape(1, n_gates), ((0, 0), (0, LANES - n_gates)))
            mixed = [_mlstm(h, gates, bias_row, batch, seq)]
            w_out = odd_w_out[j]
        xf, xb = _matmul_residual_ln(mixed, w_out.astype(BF16), xf,
                                     ln_gain[layer, 0], ln_bias[layer, 0])
        ffn = _peer(xb, peer_w_q[layer].astype(BF16), peer_sub_keys[layer].astype(BF16),
                    peer_u[layer].astype(BF16), peer_v[layer].astype(BF16))
        xf, xb = _ple_residual_ln(xf, xb, ffn, p[layer].reshape(t, PLE_DIM),
                                  ple_w_gate[layer].astype(BF16), ple_w_proj[layer].astype(BF16),
                                  ln_gain[layer, 1], ln_bias[layer, 1])
    return xf.reshape(batch, seq, dm)
```
